```python
import jax, jax.numpy as jnp
from jax import lax
import numpy as np

D_MODEL = 2048
BATCH = 2
SEQ = 8192
DEPTH = 4

N_EVEN = (DEPTH + 1) // 2
N_ODD = DEPTH // 2

A_HEAD_DIM = 64
A_HEADS = D_MODEL // 128
A_KV_HEADS = A_HEADS // 8
A_WIDTH = A_HEADS * A_HEAD_DIM
A_KV_WIDTH = A_KV_HEADS * A_HEAD_DIM
WINDOW = 128
A_BLOCK = 128
ROPE_THETA = 10000.0

B_HEAD_DIM = 128
B_HEADS = (D_MODEL // 2) // B_HEAD_DIM
B_WIDTH = B_HEADS * B_HEAD_DIM
B_CHUNK = 16
LB_FLOOR = 1e-30

C_HEAD_DIM = 128
C_HEADS = D_MODEL // C_HEAD_DIM
C_WIDTH = C_HEADS * C_HEAD_DIM
C_BLOCK = 128

EVEN_SPLITS = [A_WIDTH, A_KV_WIDTH, A_KV_WIDTH, A_WIDTH, B_WIDTH, B_WIDTH, B_WIDTH, B_WIDTH]
EVEN_IN = sum(EVEN_SPLITS)
EVEN_MIX = A_WIDTH + B_WIDTH
ODD_IN = 4 * C_WIDTH
EPS = 1e-6

kernel_name = "hybrid_swa_hgrn2_stickbreaking"


def rmsnorm(x, w):
    xf = x.astype(jnp.float32)
    return xf * lax.rsqrt(jnp.mean(xf * xf, axis=-1, keepdims=True) + EPS) * w.astype(jnp.float32)


def rope(x, pos):
    half = x.shape[-1] // 2
    inv_freq = 1.0 / (ROPE_THETA ** (jnp.arange(half, dtype=jnp.float32) / half))
    ang = pos[:, None] * inv_freq[None, :]
    cos = jnp.cos(ang)[None, :, None, :]
    sin = jnp.sin(ang)[None, :, None, :]
    x1, x2 = x[..., :half], x[..., half:]
    return jnp.concatenate([x1 * cos - x2 * sin, x2 * cos + x1 * sin], axis=-1)


def sliding_window_attention(q, k, v, sinks):
    b_, s_ = q.shape[:2]
    nb = s_ // A_BLOCK
    g = A_HEADS // A_KV_HEADS
    qb = q.reshape(b_, nb, A_BLOCK, A_KV_HEADS, g, A_HEAD_DIM)

    def band(t):
        t = t.reshape(b_, nb, A_BLOCK, A_KV_HEADS, A_HEAD_DIM)
        prev = jnp.concatenate([jnp.zeros_like(t[:, :1]), t[:, :-1]], axis=1)
        return jnp.concatenate([prev, t], axis=2)

    kb, vb = band(k), band(v)
    scores = jnp.einsum('bnqhgd,bnkhd->bnhgqk', qb, kb) * (A_HEAD_DIM ** -0.5)
    qpos = jnp.arange(nb)[:, None] * A_BLOCK + jnp.arange(A_BLOCK)[None, :]
    kpos = jnp.arange(nb)[:, None] * A_BLOCK - A_BLOCK + jnp.arange(2 * A_BLOCK)[None, :]
    diff = qpos[:, :, None] - kpos[:, None, :]
    valid = (diff >= 0) & (diff < WINDOW) & (kpos[:, None, :] >= 0)
    scores = jnp.where(valid[None, :, None, None], scores, -1e30)
    sink = jnp.broadcast_to(sinks.astype(jnp.float32).reshape(1, 1, A_KV_HEADS, g, 1, 1),
                            scores.shape[:-1] + (1,))
    probs = jax.nn.softmax(jnp.concatenate([scores, sink], axis=-1), axis=-1)[..., :-1]
    out = jnp.einsum('bnhgqk,bnkhd->bnqhgd', probs, vb)
    return out.reshape(b_, s_, A_WIDTH)


def hgrn2_chunkwise(q, logf, k, v):
    b_, s_, h_, kd = q.shape
    vd = v.shape[-1]
    nc = s_ // B_CHUNK

    def chunks(t):
        return t.reshape(b_, nc, B_CHUNK, h_, t.shape[-1]).transpose(0, 3, 1, 2, 4)

    qc, lc, kc, vc = chunks(q), chunks(logf), chunks(k), chunks(v)
    bcum = jnp.cumsum(lc, axis=3)
    causal = jnp.tril(jnp.ones((B_CHUNK, B_CHUNK), dtype=bool))
    pair = jnp.exp(jnp.minimum(bcum[:, :, :, :, None, :] - bcum[:, :, :, None, :, :], 0.0))
    att = jnp.einsum('bhnik,bhnjk,bhnijk->bhnij', qc, kc, pair)
    att = jnp.where(causal, att, 0.0)
    o_intra = jnp.einsum('bhnij,bhnjv->bhniv', att, vc)

    blast = bcum[:, :, :, -1:]
    u = jnp.einsum('bhnjk,bhnjv->bhnkv', kc * jnp.exp(blast - bcum), vc)
    decay = jnp.exp(blast[:, :, :, 0])

    def step(state, inp):
        d, inc = inp
        return d[..., None] * state + inc, state

    s0 = jnp.zeros((b_, h_, kd, vd), jnp.float32)
    _, s_prev = lax.scan(step, s0, (decay.transpose(2, 0, 1, 3), u.transpose(2, 0, 1, 3, 4)))
    s_prev = s_prev.transpose(1, 2, 0, 3, 4)
    o_inter = jnp.einsum('bhnik,bhnkv->bhniv', qc * jnp.exp(bcum), s_prev)
    o = o_intra + o_inter
    return o.transpose(0, 2, 3, 1, 4).reshape(b_, s_, h_, vd)


def stick_breaking_attention(q, k, v):
    b_, h_, s_, d_ = q.shape
    nb = s_ // C_BLOCK
    qb = q.reshape(b_, h_, nb, C_BLOCK, d_).transpose(2, 0, 1, 3, 4)
    kpos = jnp.arange(s_)
    scale = d_ ** -0.5

    def block(args):
        c, qblk = args
        qpos = c * C_BLOCK + jnp.arange(C_BLOCK)
        z = jnp.einsum('bhqd,bhkd->bhqk', qblk, k) * scale
        before = kpos[None, :] < qpos[:, None]
        log1m = jnp.where(before, jax.nn.log_sigmoid(-z), 0.0)
        rc = lax.cumsum(log1m, axis=3, reverse=True)
        rc_excl = jnp.concatenate([rc[..., 1:], jnp.zeros_like(rc[..., :1])], axis=-1)
        w = jnp.where(before, jnp.exp(jax.nn.log_sigmoid(z) + rc_excl), 0.0)
        return jnp.einsum('bhqk,bhkd->bhqd', w, v)

    o = lax.map(block, (jnp.arange(nb), qb))
    return o.transpose(1, 0, 3, 2, 4).reshape(b_, s_, h_ * d_)


def even_layer(x, pos, ln, w_in, qn, kn, sinks, lb, gn, w_out):
    b_, s_, _ = x.shape
    h = rmsnorm(x, ln).astype(x.dtype)
    proj = h @ w_in
    idx = list(np.cumsum(EVEN_SPLITS)[:-1])
    aq, ak, av, ag, bq, bf, bi, bg = jnp.split(proj, idx, axis=-1)

    aq = rope(rmsnorm(aq.reshape(b_, s_, A_HEADS, A_HEAD_DIM), qn), pos)
    ak = rope(rmsnorm(ak.reshape(b_, s_, A_KV_HEADS, A_HEAD_DIM), kn), pos)
    av = av.reshape(b_, s_, A_KV_HEADS, A_HEAD_DIM).astype(jnp.float32)
    ya = sliding_window_attention(aq, ak, av, sinks) * jax.nn.silu(ag.astype(jnp.float32))

    lbf = lb.astype(jnp.float32)
    logf = jnp.logaddexp(jnp.log(jnp.maximum(lbf, LB_FLOOR)),
                         jnp.log1p(-lbf) + jax.nn.log_sigmoid(bf.astype(jnp.float32)))
    kin = -jnp.expm1(logf)
    shp = (b_, s_, B_HEADS, B_HEAD_DIM)
    ob = hgrn2_chunkwise(jax.nn.silu(bq.astype(jnp.float32)).reshape(shp), logf.reshape(shp),
                         kin.reshape(shp), bi.astype(jnp.float32).reshape(shp))
    yb = rmsnorm(ob, gn) * jax.nn.silu(bg.astype(jnp.float32)).reshape(shp)
    yb = yb.reshape(b_, s_, B_WIDTH)

    y = jnp.concatenate([ya, yb], axis=-1).astype(x.dtype) @ w_out
    return x + y


def odd_layer(x, ln, w_in, w_out):
    b_, s_, _ = x.shape
    h = rmsnorm(x, ln).astype(x.dtype)
    cq, ck, cv, cg = jnp.split(h @ w_in, 4, axis=-1)

    def heads(t):
        return t.reshape(b_, s_, C_HEADS, C_HEAD_DIM).transpose(0, 2, 1, 3).astype(jnp.float32)

    oc = stick_breaking_attention(heads(cq), heads(ck), heads(cv))
    y = (oc * jax.nn.silu(cg.astype(jnp.float32))).astype(x.dtype) @ w_out
    return x + y


def setup_inputs(seed: int = 0) -> dict:
    key = jax.random.key(seed)
    ks = jax.random.split(key, 12)
    f32 = jnp.float32
    return {
        "x": jax.random.normal(ks[0], (BATCH, SEQ, D_MODEL), f32),
        "ln_even": 1.0 + 0.02 * jax.random.normal(ks[1], (N_EVEN, D_MODEL), f32),
        "w_in_even": jax.random.normal(ks[2], (N_EVEN, D_MODEL, EVEN_IN), f32) * D_MODEL ** -0.5,
        "q_norm_a": 1.0 + 0.02 * jax.random.normal(ks[3], (N_EVEN, A_HEAD_DIM), f32),
        "k_norm_a": 1.0 + 0.02 * jax.random.normal(ks[4], (N_EVEN, A_HEAD_DIM), f32),
        "sinks_a": jax.random.normal(ks[5], (N_EVEN, A_HEADS), f32),
        "lower_bounds": 0.5 * jax.random.normal(ks[6], (N_EVEN, B_WIDTH), f32),
        "g_norm_b": 1.0 + 0.02 * jax.random.normal(ks[7], (N_EVEN, B_HEAD_DIM), f32),
        "w_out_even": jax.random.normal(ks[8], (N_EVEN, EVEN_MIX, D_MODEL), f32) * EVEN_MIX ** -0.5,
        "ln_odd": 1.0 + 0.02 * jax.random.normal(ks[9], (N_ODD, D_MODEL), f32),
        "w_in_odd": jax.random.normal(ks[10], (N_ODD, D_MODEL, ODD_IN), f32) * D_MODEL ** -0.5,
        "w_out_odd": jax.random.normal(ks[11], (N_ODD, C_WIDTH, D_MODEL), f32) * C_WIDTH ** -0.5,
    }


def reference(x, ln_even, w_in_even, q_norm_a, k_norm_a, sinks_a, lower_bounds, g_norm_b,
              w_out_even, ln_odd, w_in_odd, w_out_odd):
    pos = jnp.arange(x.shape[1], dtype=jnp.float32)
    lbs = jax.nn.softmax(lower_bounds.astype(jnp.float32), axis=0)
    lbs = jnp.cumsum(lbs, axis=0) - lbs[0]
    for layer in range(DEPTH):
        if layer % 2 == 0:
            e = layer // 2
            x = even_layer(x, pos, ln_even[e], w_in_even[e], q_norm_a[e], k_norm_a[e], sinks_a[e],
                           lbs[e], g_norm_b[e], w_out_even[e])
        else:
            o = layer // 2
            x = odd_layer(x, ln_odd[o], w_in_odd[o], w_out_odd[o])
    return x
```

```python
import functools
import math

import jax
import jax.numpy as jnp
import numpy as np
from jax import lax
from jax.experimental import pallas as pl
from jax.experimental.pallas import tpu as pltpu

F32 = jnp.float32
BF16 = jnp.bfloat16

EPS = 1e-6
ROPE_THETA = 10000.0
LB_FLOOR = 1e-30
NEG_BIG = -1e30

V7X_VMEM_BYTES = 64 * 1024 * 1024
VMEM_LIMIT = V7X_VMEM_BYTES - 8 * 1024 * 1024
LANES = 128

A_HEAD_DIM = 64
A_HALF = A_HEAD_DIM // 2
A_HEADS = 16
A_KV_HEADS = 2
A_GROUP = A_HEADS // A_KV_HEADS
A_PAIRS = A_HEADS // 2
A_WIDTH = A_HEADS * A_HEAD_DIM
A_BLOCK = 128
B_HEAD_DIM = 128
B_HEADS = 8
B_WIDTH = B_HEADS * B_HEAD_DIM
B_BLOCK = 128
B_LEVELS = int(math.log2(B_BLOCK))
C_HEAD_DIM = 128
C_HEADS = 16
C_WIDTH = C_HEADS * C_HEAD_DIM
C_TILE = 256
LOG2E = 1.0 / math.log(2.0)


def _cparams(semantics):
    return pltpu.CompilerParams(dimension_semantics=semantics, vmem_limit_bytes=VMEM_LIMIT)


def _split_bf16(x):
    hi = x.astype(BF16)
    lo = (x - hi.astype(F32)).astype(BF16)
    return hi, lo


def _dot(a, b):
    return jnp.dot(a, b, preferred_element_type=F32)


def _dot_nt(a, b):
    return lax.dot_general(a, b, (((1,), (1,)), ((), ())), preferred_element_type=F32)


def _sigmoid_pair(x):
    e = jnp.exp(-jnp.abs(x))
    r = 1.0 / (1.0 + e)
    er = e * r
    pos = x >= 0
    return jnp.where(pos, r, er), jnp.where(pos, er, r)


def _silu(x):
    return x * _sigmoid_pair(x)[0]


def _inproj_body(x_ref, ln_ref, w_ref, cs_ref, *rest, seg_tiles):
    out_refs, h_ref = rest[:len(seg_tiles)], rest[len(seg_tiles)]
    j = pl.program_id(1)

    @pl.when(j == 0)
    def _():
        x = x_ref[...]
        ms = jnp.mean(x * x, axis=-1, keepdims=True)
        h_ref[...] = (x * lax.rsqrt(ms + EPS) * ln_ref[...]).astype(BF16)

    acc = _dot(h_ref[...], w_ref[...]) * cs_ref[...]
    start = 0
    for o_ref, nt in zip(out_refs, seg_tiles):
        @pl.when((j >= start) & (j < start + nt))
        def _(o_ref=o_ref):
            o_ref[...] = acc.astype(o_ref.dtype)
        start += nt


def _inproj(x2, ln, w_bf16, colscale, segs, tm, tn):
    t, d = x2.shape
    n = w_bf16.shape[1]
    seg_tiles = tuple(wd // tn for wd, _ in segs)
    assert sum(seg_tiles) * tn == n and t % tm == 0
    out_specs, out_shapes, start = [], [], 0
    for (wd, dt), nt in zip(segs, seg_tiles):
        out_specs.append(pl.BlockSpec(
            (tm, tn), lambda i, j, s=start, m=nt - 1: (i, jnp.clip(j - s, 0, m))))
        out_shapes.append(jax.ShapeDtypeStruct((t, wd), dt))
        start += nt
    return pl.pallas_call(
        functools.partial(_inproj_body, seg_tiles=seg_tiles),
        grid=(t // tm, n // tn),
        in_specs=[
            pl.BlockSpec((tm, d), lambda i, j: (i, 0)),
            pl.BlockSpec((1, d), lambda i, j: (0, 0)),
            pl.BlockSpec((d, tn), lambda i, j: (0, j)),
            pl.BlockSpec((1, tn), lambda i, j: (0, j)),
        ],
        out_specs=out_specs,
        out_shape=out_shapes,
        scratch_shapes=[pltpu.VMEM((tm, d), BF16)],
        compiler_params=_cparams(("parallel", "arbitrary")),
        name="norm_inproj",
    )(x2, ln.reshape(1, d), w_bf16, colscale)


def _outproj_body(ya_ref, yb_ref, wa_ref, wb_ref, x_ref, o_ref):
    acc = _dot(ya_ref[...], wa_ref[...]) + _dot(yb_ref[...], wb_ref[...])
    o_ref[...] = x_ref[...] + acc


def _outproj(ya, yb, a_col, b_col, w_bf16, x2, tm):
    t, d = x2.shape
    half = w_bf16.shape[0] // 2
    return pl.pallas_call(
        _outproj_body,
        grid=(t // tm,),
        in_specs=[
            pl.BlockSpec((tm, half), lambda i: (i, a_col)),
            pl.BlockSpec((tm, half), lambda i: (i, b_col)),
            pl.BlockSpec((half, d), lambda i: (0, 0)),
            pl.BlockSpec((half, d), lambda i: (1, 0)),
            pl.BlockSpec((tm, d), lambda i: (i, 0)),
        ],
        out_specs=pl.BlockSpec((tm, d), lambda i: (i, 0)),
        out_shape=jax.ShapeDtypeStruct((t, d), F32),
        compiler_params=_cparams(("parallel",)),
        name="outproj_residual",
    )(ya, yb, w_bf16, w_bf16, x2)


def _swa_body(sink_ref, q_ref, g_ref, kvp_ref, kvc_ref, cosp_ref, sinp_ref, cosc_ref, sinc_ref,
              qn_ref, kn_ref, grp_ref, o_ref):
    i = pl.program_id(1)
    blk = q_ref.shape[0]
    lane = lax.broadcasted_iota(jnp.int32, (blk, LANES), 1)
    row = lax.broadcasted_iota(jnp.int32, (blk, 2 * blk), 0)
    col = lax.broadcasted_iota(jnp.int32, (blk, 2 * blk), 1)
    no_prev = jnp.where(i > 0, 0, 2 * blk)
    valid = ((col < blk) & (col > row + no_prev)) | ((col >= blk) & (col - blk <= row))
    cos_c, sin_c = cosc_ref[...], sinc_ref[...]

    def rope(x, cos, sin):
        return x * cos + pltpu.roll(x, 2 * A_HALF, axis=1) * sin

    def keys(kv_ref, cos, sin, g):
        k = kv_ref[:, g * LANES:(g + 1) * LANES]
        ms = jnp.sum(k * k, axis=-1, keepdims=True) * (1.0 / LANES)
        k = rope(k * lax.rsqrt(ms + EPS) * kn_ref[...], cos, sin)
        v = kv_ref[:, (A_KV_HEADS + g) * LANES:(A_KV_HEADS + g + 1) * LANES]
        return k.astype(BF16), v.astype(BF16)

    band = []
    for g in range(A_KV_HEADS):
        kp, vp = keys(kvp_ref, cosp_ref[...], sinp_ref[...], g)
        kc, vc = keys(kvc_ref, cos_c, sin_c, g)
        band.append((jnp.concatenate([kp, kc], axis=0), jnp.concatenate([vp, vc], axis=0)))

    grp = grp_ref[...]
    for p in range(A_PAIRS):
        kb, vb = band[(2 * p) // A_GROUP]
        q = q_ref[:, p * LANES:(p + 1) * LANES]
        sq_hi, sq_lo = _split_bf16(q * q)
        ms = (_dot(sq_hi, grp) + _dot(sq_lo, grp)) * (1.0 / A_HEAD_DIM)
        q = rope(q * lax.rsqrt(ms + EPS) * qn_ref[...], cos_c, sin_c)
        outs = []
        for hh in range(2):
            qh = jnp.where((lane // A_HALF) % 2 == hh, q, 0.0).astype(BF16)
            s = jnp.where(valid, _dot_nt(qh, kb), NEG_BIG)
            sink = sink_ref[2 * p + hh]
            m = jnp.maximum(jnp.max(s, axis=-1, keepdims=True), sink)
            e = jnp.exp(s - m)
            denom = jnp.sum(e, axis=-1, keepdims=True) + jnp.exp(sink - m)
            outs.append(_dot(e.astype(BF16), vb) / denom)
        o = jnp.where(lane < A_HEAD_DIM, outs[0], outs[1])
        o_ref[:, p * LANES:(p + 1) * LANES] = (o * _silu(g_ref[:, p * LANES:(p + 1) * LANES])).astype(o_ref.dtype)


def _swa(proj, sinks, cos_t, sin_t, qn_l, kn_l, grp, batch, seq, cols):
    nb = seq // A_BLOCK
    q_col, g_col, kv_col = cols
    kv_w = 2 * A_KV_HEADS * LANES
    row = lambda b, i: b * nb + i
    prev = lambda i: jnp.maximum(i - 1, 0)
    return pl.pallas_call(
        _swa_body,
        grid=(batch, nb),
        in_specs=[
            pl.BlockSpec(memory_space=pltpu.SMEM),
            pl.BlockSpec((A_BLOCK, A_WIDTH), lambda b, i: (row(b, i), q_col)),
            pl.BlockSpec((A_BLOCK, A_WIDTH), lambda b, i: (row(b, i), g_col)),
            pl.BlockSpec((A_BLOCK, kv_w), lambda b, i: (row(b, prev(i)), kv_col)),
            pl.BlockSpec((A_BLOCK, kv_w), lambda b, i: (row(b, i), kv_col)),
            pl.BlockSpec((A_BLOCK, LANES), lambda b, i: (prev(i), 0)),
            pl.BlockSpec((A_BLOCK, LANES), lambda b, i: (prev(i), 0)),
            pl.BlockSpec((A_BLOCK, LANES), lambda b, i: (i, 0)),
            pl.BlockSpec((A_BLOCK, LANES), lambda b, i: (i, 0)),
            pl.BlockSpec((1, LANES), lambda b, i: (0, 0)),
            pl.BlockSpec((1, LANES), lambda b, i: (0, 0)),
            pl.BlockSpec((LANES, LANES), lambda b, i: (0, 0)),
        ],
        out_specs=pl.BlockSpec((A_BLOCK, A_WIDTH), lambda b, i: (row(b, i), 0)),
        out_shape=jax.ShapeDtypeStruct((batch * seq, A_WIDTH), BF16),
        compiler_params=_cparams(("parallel", "parallel")),
        name="swa_mixer",
    )(sinks, proj, proj, proj, proj, cos_t, sin_t, cos_t, sin_t, qn_l, kn_l, grp)


def _hgrn_body(q_ref, f_ref, v_ref, g_ref, lbp_ref, gn_ref, cum_ref, lvl_ref, o_ref, st_ref, *, layer):
    @pl.when(pl.program_id(1) == 0)
    def _():
        st_ref[...] = jnp.zeros_like(st_ref)

    lbp = lbp_ref[...]
    ex = jnp.exp(lbp - jnp.max(lbp, axis=0, keepdims=True))
    sm = ex / jnp.sum(ex, axis=0, keepdims=True)
    lb = jnp.sum(sm[:layer + 1], axis=0, keepdims=True) - sm[0:1]
    lbc = jnp.maximum(lb, LB_FLOOR)
    oml = 1.0 - lb

    s_pos, s_neg = _sigmoid_pair(f_ref[...])
    logf = jnp.log(lbc + oml * s_pos)
    kin = oml * s_neg - (lbc - lb)
    q = _silu(q_ref[...])

    lf_hi, lf_lo = _split_bf16(logf)
    cum = cum_ref[...]
    sums = _dot(cum, lf_hi) + _dot(cum, lf_lo)
    blk = q.shape[0]
    seg = lambda n: sums[n * blk:(n + 1) * blk]
    b_incl, b_rest = seg(2 * B_LEVELS - 2), seg(2 * B_LEVELS - 1)
    b_last = b_incl[blk - 1:blk]

    qs = [(q * jnp.exp(logf)).astype(BF16)]
    ks = [kin.astype(BF16)]
    for lv in range(1, B_LEVELS):
        qs.append((q * jnp.exp(seg(2 * lv - 2))).astype(BF16))
        ks.append((kin * jnp.exp(seg(2 * lv - 1))).astype(BF16))
    qs.append(q.astype(BF16))
    ks.append(ks[0])
    q_state = (q * jnp.exp(b_incl)).astype(BF16)
    k_state = (kin * jnp.exp(b_rest)).astype(BF16)
    decay = jnp.exp(b_last)

    lvl = lvl_ref[...]
    v = v_ref[...]
    for h in range(B_HEADS):
        sl = slice(h * B_HEAD_DIM, (h + 1) * B_HEAD_DIM)
        att = jnp.zeros((blk, blk), F32)
        for n in range(B_LEVELS + 1):
            att = jnp.where(lvl == n, _dot_nt(qs[n][:, sl], ks[n][:, sl]), att)
        vh = v[:, sl]
        st = st_ref[h]
        o = _dot(att.astype(BF16), vh.astype(BF16)) + _dot_nt(q_state[:, sl], st.astype(BF16))
        st_ref[h] = st * decay[:, sl] + _dot(vh.T.astype(BF16), k_state[:, sl])
        ms = jnp.mean(o * o, axis=-1, keepdims=True)
        y = o * lax.rsqrt(ms + EPS) * gn_ref[...] * _silu(g_ref[:, sl])
        o_ref[:, sl] = y.astype(o_ref.dtype)


def _hgrn_constants():
    blk = B_BLOCK
    i = np.arange(blk)[:, None]
    j = np.arange(blk)[None, :]
    mats = []
    lvl = np.full((blk, blk), -1, np.int32)
    lvl[i == j] = B_LEVELS
    for lv in range(B_LEVELS):
        s = 1 << lv
        same = (i // s) == (j // s)
        if lv > 0:
            mats += [same & (j <= i), same & (j > i)]
        lvl[((i // s) % 2 == 1) & ((j // s) == (i // s) - 1)] = lv
    mats += [j <= i, j > i]
    return np.concatenate(mats, axis=0).astype(np.float32), lvl


def _hgrn(proj, lower_bounds, gn, batch, seq, cols, layer):
    nb = seq // B_BLOCK
    cum, lvl = _hgrn_constants()
    row = lambda b, i: b * nb + i
    tile = lambda c: pl.BlockSpec((B_BLOCK, B_WIDTH), lambda b, i: (row(b, i), c))
    whole = lambda a: pl.BlockSpec(a.shape, lambda b, i: (0,) * a.ndim)
    gn2 = gn.reshape(1, B_HEAD_DIM)
    cum = jnp.asarray(cum, BF16)
    lvl = jnp.asarray(lvl)
    return pl.pallas_call(
        functools.partial(_hgrn_body, layer=layer),
        grid=(batch, nb),
        in_specs=[tile(cols[0]), tile(cols[1]), tile(cols[2]), tile(cols[3]),
                  whole(lower_bounds), whole(gn2), whole(cum), whole(lvl)],
        out_specs=pl.BlockSpec((B_BLOCK, B_WIDTH), lambda b, i: (row(b, i), 0)),
        out_shape=jax.ShapeDtypeStruct((batch * seq, B_WIDTH), BF16),
        scratch_shapes=[pltpu.VMEM((B_HEADS, B_HEAD_DIM, B_HEAD_DIM), F32)],
        compiler_params=_cparams(("parallel", "arbitrary")),
        name="hgrn2_mixer",
    )(proj, proj, proj, proj, lower_bounds, gn2, cum, lvl)


def _stick_tile(z2, v_blk, tri, carry, mask):
    e = jnp.exp2(-jnp.abs(z2))
    p = jnp.maximum(z2, 0.0) + jnp.log(1.0 + e) * LOG2E
    log_beta = z2 - p
    if mask is not None:
        p = jnp.where(mask, p, 0.0)
    p_hi, p_lo = _split_bf16(p)
    later = _dot(p_hi, tri) + _dot(p_lo, tri)
    w = jnp.exp2(log_beta - later - carry)
    if mask is not None:
        w = jnp.where(mask, w, 0.0)
    return _dot(w.astype(BF16), v_blk), jnp.sum(p, axis=-1, keepdims=True)


def _stick_body(q_ref, k_ref, v_ref, g_ref, tri_ref, o_ref):
    i = pl.program_id(2)
    t = q_ref.shape[0]
    q = q_ref[...]
    tri = tri_ref[...]
    row = lax.broadcasted_iota(jnp.int32, (t, t), 0)
    col = lax.broadcasted_iota(jnp.int32, (t, t), 1)

    def tile(kb, carry, mask):
        start = pl.multiple_of(kb * t, t)
        z2 = _dot_nt(q, k_ref[pl.ds(start, t), :])
        return _stick_tile(z2, v_ref[pl.ds(start, t), :], tri, carry, mask)

    acc, carry = tile(i, jnp.zeros((t, 1), F32), col < row)

    def step(n, state):
        acc, carry = state
        o, tot = tile(i - 1 - n, carry, None)
        return acc + o, carry + tot

    acc, _ = lax.fori_loop(0, i, step, (acc, carry))
    o_ref[...] = (acc * _silu(g_ref[...])).astype(o_ref.dtype)


def _stick(qkv, gate, batch, seq):
    t = C_TILE
    nq = seq // t
    tri = jnp.asarray(np.arange(t)[:, None] > np.arange(t)[None, :], BF16)
    row = lambda b, h, i: b * nq + i
    return pl.pallas_call(
        _stick_body,
        grid=(batch, C_HEADS, nq),
        in_specs=[
            pl.BlockSpec((t, C_HEAD_DIM), lambda b, h, i: (row(b, h, i), h)),
            pl.BlockSpec((seq, C_HEAD_DIM), lambda b, h, i: (b, C_HEADS + h)),
            pl.BlockSpec((seq, C_HEAD_DIM), lambda b, h, i: (b, 2 * C_HEADS + h)),
            pl.BlockSpec((t, C_HEAD_DIM), lambda b, h, i: (row(b, h, i), h)),
            pl.BlockSpec((t, t), lambda b, h, i: (0, 0)),
        ],
        out_specs=pl.BlockSpec((t, C_HEAD_DIM), lambda b, h, i: (row(b, h, i), h)),
        out_shape=jax.ShapeDtypeStruct((batch * seq, C_WIDTH), BF16),
        compiler_params=_cparams(("parallel", "parallel", "arbitrary")),
        name="stick_breaking_mixer",
    )(qkv, qkv, qkv, gate, tri)


def _even_layout():
    lane = np.arange(LANES)
    dim = lane % A_HALF + A_HALF * (lane // (2 * A_HALF))
    aq = np.concatenate([(2 * p + (lane // A_HALF) % 2) * A_HEAD_DIM + dim for p in range(A_PAIRS)])
    kv0 = A_WIDTH
    ak = np.concatenate([kv0 + g * A_HEAD_DIM + dim for g in range(A_KV_HEADS)])
    v0 = kv0 + A_KV_HEADS * A_HEAD_DIM
    av = np.concatenate([v0 + g * A_HEAD_DIM + lane % A_HEAD_DIM for g in range(A_KV_HEADS)])
    rest0 = v0 + A_KV_HEADS * A_HEAD_DIM
    rest = rest0 + np.arange(A_WIDTH + 4 * B_WIDTH)
    cols = np.concatenate([aq, rest, ak, av])
    grp = ((lane[:, None] // A_HALF) % 2 == (lane[None, :] // A_HALF) % 2).astype(np.float32)
    return cols, dim, grp


def _rope_tables(seq):
    half = A_HALF
    inv_freq = 1.0 / (ROPE_THETA ** (jnp.arange(half, dtype=F32) / half))
    ang = jnp.arange(seq, dtype=F32)[:, None] * inv_freq[None, :]
    cos, sin = jnp.cos(ang), jnp.sin(ang)
    return jnp.tile(cos, (1, 4)), jnp.concatenate([-sin, -sin, sin, sin], axis=1)


def _even_layer(x2, batch, seq, ln, w_in, qn, kn, sinks, lower_bounds, gn, w_out, layer, tables):
    cols, dim, grp = _even_layout()
    w = w_in[:, cols].astype(BF16)
    n = w.shape[1]
    tn = n // 4
    proj, = _inproj(x2, ln, w, jnp.ones((1, n), F32), [(n, F32)], tm=512, tn=tn)
    qn_l = (qn[dim] * A_HEAD_DIM ** -0.5).reshape(1, LANES)
    kn_l = kn[dim].reshape(1, LANES)
    kv_col = (2 * A_WIDTH + 4 * B_WIDTH) // (2 * A_KV_HEADS * LANES)
    ya = _swa(proj, sinks, tables[0], tables[1], qn_l, kn_l, jnp.asarray(grp, BF16),
              batch, seq, (0, 1, kv_col))
    yb = _hgrn(proj, lower_bounds, gn, batch, seq, (2, 3, 4, 5), layer)
    return _outproj(ya, yb, 0, 0, w_out.astype(BF16), x2, tm=512)


def _odd_layer(x2, batch, seq, ln, w_in, w_out):
    n = w_in.shape[1]
    qscale = C_HEAD_DIM ** -0.5 * LOG2E
    colscale = jnp.concatenate([jnp.full((1, C_WIDTH), qscale, F32), jnp.ones((1, n - C_WIDTH), F32)], axis=1)
    qkv, gate = _inproj(x2, ln, w_in.astype(BF16), colscale,
                        [(3 * C_WIDTH, BF16), (C_WIDTH, F32)], tm=512, tn=C_WIDTH)
    y = _stick(qkv, gate, batch, seq)
    return _outproj(y, y, 0, 1, w_out.astype(BF16), x2, tm=512)


def kernel(x, ln_even, w_in_even, q_norm_a, k_norm_a, sinks_a, lower_bounds, g_norm_b, w_out_even,
           ln_odd, w_in_odd, w_out_odd):
    batch, seq, d = x.shape
    depth = ln_even.shape[0] + ln_odd.shape[0]
    x2 = x.reshape(batch * seq, d)
    tables = _rope_tables(seq)
    for layer in range(depth):
        e = layer // 2
        if layer % 2 == 0:
            x2 = _even_layer(x2, batch, seq, ln_even[e], w_in_even[e], q_norm_a[e], k_norm_a[e],
                             sinks_a[e], lower_bounds, g_norm_b[e], w_out_even[e], e, tables)
        else:
            x2 = _odd_layer(x2, batch, seq, ln_odd[e], w_in_odd[e], w_out_odd[e])
    return x2.reshape(batch, seq, d)
```

```python
import functools
import math

import jax
import jax.numpy as jnp
import numpy as np
from jax import lax
from jax.experimental import pallas as pl
from jax.experimental.pallas import tpu as pltpu

F32 = jnp.float32
BF16 = jnp.bfloat16

EPS = 1e-6
ROPE_THETA = 10000.0
LB_FLOOR = 1e-30
NEG_BIG = -1e30

V7X_VMEM_BYTES = 64 * 1024 * 1024
VMEM_LIMIT = V7X_VMEM_BYTES - 8 * 1024 * 1024
LANES = 128

A_HEAD_DIM = 64
A_HALF = A_HEAD_DIM // 2
A_HEADS = 16
A_KV_HEADS = 2
A_GROUP = A_HEADS // A_KV_HEADS
A_PAIRS = A_HEADS // 2
A_WIDTH = A_HEADS * A_HEAD_DIM
A_BLOCK = 128
B_HEAD_DIM = 128
B_HEADS = 8
B_WIDTH = B_HEADS * B_HEAD_DIM
B_BLOCK = 128
B_LEVELS = int(math.log2(B_BLOCK))
C_HEAD_DIM = 128
C_HEADS = 16
C_WIDTH = C_HEADS * C_HEAD_DIM
C_TILE = 256
LOG2E = 1.0 / math.log(2.0)
C_DONE_BITS = 150.0


def _cparams(semantics):
    return pltpu.CompilerParams(dimension_semantics=semantics, vmem_limit_bytes=VMEM_LIMIT)


def _split_bf16(x):
    hi = x.astype(BF16)
    lo = (x - hi.astype(F32)).astype(BF16)
    return hi, lo


def _dot(a, b):
    return jnp.dot(a, b, preferred_element_type=F32)


def _dot_nt(a, b):
    return lax.dot_general(a, b, (((1,), (1,)), ((), ())), preferred_element_type=F32)


def _sigmoid_pair(x):
    e = jnp.exp(-jnp.abs(x))
    r = 1.0 / (1.0 + e)
    er = e * r
    pos = x >= 0
    return jnp.where(pos, r, er), jnp.where(pos, er, r)


def _silu(x):
    return x * _sigmoid_pair(x)[0]


def _inproj_body(x_ref, ln_ref, w_ref, cs_ref, *rest, seg_tiles):
    out_refs, h_ref = rest[:len(seg_tiles)], rest[len(seg_tiles)]
    j = pl.program_id(1)

    @pl.when(j == 0)
    def _():
        x = x_ref[...]
        ms = jnp.mean(x * x, axis=-1, keepdims=True)
        h_ref[...] = (x * lax.rsqrt(ms + EPS) * ln_ref[...]).astype(BF16)

    acc = _dot(h_ref[...], w_ref[...]) * cs_ref[...]
    start = 0
    for o_ref, nt in zip(out_refs, seg_tiles):
        @pl.when((j >= start) & (j < start + nt))
        def _(o_ref=o_ref):
            o_ref[...] = acc.astype(o_ref.dtype)
        start += nt


def _inproj(x2, ln, w_bf16, colscale, segs, tm, tn):
    t, d = x2.shape
    n = w_bf16.shape[1]
    seg_tiles = tuple(wd // tn for wd, _ in segs)
    assert sum(seg_tiles) * tn == n and t % tm == 0
    out_specs, out_shapes, start = [], [], 0
    for (wd, dt), nt in zip(segs, seg_tiles):
        out_specs.append(pl.BlockSpec(
            (tm, tn), lambda i, j, s=start, m=nt - 1: (i, jnp.clip(j - s, 0, m))))
        out_shapes.append(jax.ShapeDtypeStruct((t, wd), dt))
        start += nt
    return pl.pallas_call(
        functools.partial(_inproj_body, seg_tiles=seg_tiles),
        grid=(t // tm, n // tn),
        in_specs=[
            pl.BlockSpec((tm, d), lambda i, j: (i, 0)),
            pl.BlockSpec((1, d), lambda i, j: (0, 0)),
            pl.BlockSpec((d, tn), lambda i, j: (0, j)),
            pl.BlockSpec((1, tn), lambda i, j: (0, j)),
        ],
        out_specs=out_specs,
        out_shape=out_shapes,
        scratch_shapes=[pltpu.VMEM((tm, d), BF16)],
        compiler_params=_cparams(("parallel", "arbitrary")),
        name="norm_inproj",
    )(x2, ln.reshape(1, d), w_bf16, colscale)


def _outproj_body(ya_ref, yb_ref, wa_ref, wb_ref, x_ref, o_ref):
    acc = _dot(ya_ref[...], wa_ref[...]) + _dot(yb_ref[...], wb_ref[...])
    o_ref[...] = x_ref[...] + acc


def _outproj(ya, yb, a_col, b_col, w_bf16, x2, tm):
    t, d = x2.shape
    half = w_bf16.shape[0] // 2
    return pl.pallas_call(
        _outproj_body,
        grid=(t // tm,),
        in_specs=[
            pl.BlockSpec((tm, half), lambda i: (i, a_col)),
            pl.BlockSpec((tm, half), lambda i: (i, b_col)),
            pl.BlockSpec((half, d), lambda i: (0, 0)),
            pl.BlockSpec((half, d), lambda i: (1, 0)),
            pl.BlockSpec((tm, d), lambda i: (i, 0)),
        ],
        out_specs=pl.BlockSpec((tm, d), lambda i: (i, 0)),
        out_shape=jax.ShapeDtypeStruct((t, d), F32),
        compiler_params=_cparams(("parallel",)),
        name="outproj_residual",
    )(ya, yb, w_bf16, w_bf16, x2)


def _swa_body(sink_ref, q_ref, g_ref, kvp_ref, kvc_ref, cosp_ref, sinp_ref, cosc_ref, sinc_ref,
              qn_ref, kn_ref, grp_ref, o_ref):
    i = pl.program_id(1)
    blk = q_ref.shape[0]
    lane = lax.broadcasted_iota(jnp.int32, (blk, LANES), 1)
    row = lax.broadcasted_iota(jnp.int32, (blk, 2 * blk), 0)
    col = lax.broadcasted_iota(jnp.int32, (blk, 2 * blk), 1)
    no_prev = jnp.where(i > 0, 0, 2 * blk)
    valid = ((col < blk) & (col > row + no_prev)) | ((col >= blk) & (col - blk <= row))
    cos_c, sin_c = cosc_ref[...], sinc_ref[...]

    def rope(x, cos, sin):
        return x * cos + pltpu.roll(x, 2 * A_HALF, axis=1) * sin

    def keys(kv_ref, cos, sin, g):
        k = kv_ref[:, g * LANES:(g + 1) * LANES]
        ms = jnp.sum(k * k, axis=-1, keepdims=True) * (1.0 / LANES)
        k = rope(k * lax.rsqrt(ms + EPS) * kn_ref[...], cos, sin)
        v = kv_ref[:, (A_KV_HEADS + g) * LANES:(A_KV_HEADS + g + 1) * LANES]
        return k.astype(BF16), v.astype(BF16)

    band = []
    for g in range(A_KV_HEADS):
        kp, vp = keys(kvp_ref, cosp_ref[...], sinp_ref[...], g)
        kc, vc = keys(kvc_ref, cos_c, sin_c, g)
        band.append((jnp.concatenate([kp, kc], axis=0), jnp.concatenate([vp, vc], axis=0)))

    grp = grp_ref[...]
    for p in range(A_PAIRS):
        kb, vb = band[(2 * p) // A_GROUP]
        q = q_ref[:, p * LANES:(p + 1) * LANES]
        sq_hi, sq_lo = _split_bf16(q * q)
        ms = (_dot(sq_hi, grp) + _dot(sq_lo, grp)) * (1.0 / A_HEAD_DIM)
        q = rope(q * lax.rsqrt(ms + EPS) * qn_ref[...], cos_c, sin_c)
        outs = []
        for hh in range(2):
            qh = jnp.where((lane // A_HALF) % 2 == hh, q, 0.0).astype(BF16)
            s = jnp.where(valid, _dot_nt(qh, kb), NEG_BIG)
            sink = sink_ref[2 * p + hh]
            m = jnp.maximum(jnp.max(s, axis=-1, keepdims=True), sink)
            e = jnp.exp(s - m)
            denom = jnp.sum(e, axis=-1, keepdims=True) + jnp.exp(sink - m)
            outs.append(_dot(e.astype(BF16), vb) / denom)
        o = jnp.where(lane < A_HEAD_DIM, outs[0], outs[1])
        o_ref[:, p * LANES:(p + 1) * LANES] = (o * _silu(g_ref[:, p * LANES:(p + 1) * LANES])).astype(o_ref.dtype)


def _swa(proj, sinks, cos_t, sin_t, qn_l, kn_l, grp, batch, seq, cols):
    nb = seq // A_BLOCK
    q_col, g_col, kv_col = cols
    kv_w = 2 * A_KV_HEADS * LANES
    row = lambda b, i: b * nb + i
    prev = lambda i: jnp.maximum(i - 1, 0)
    return pl.pallas_call(
        _swa_body,
        grid=(batch, nb),
        in_specs=[
            pl.BlockSpec(memory_space=pltpu.SMEM),
            pl.BlockSpec((A_BLOCK, A_WIDTH), lambda b, i: (row(b, i), q_col)),
            pl.BlockSpec((A_BLOCK, A_WIDTH), lambda b, i: (row(b, i), g_col)),
            pl.BlockSpec((A_BLOCK, kv_w), lambda b, i: (row(b, prev(i)), kv_col)),
            pl.BlockSpec((A_BLOCK, kv_w), lambda b, i: (row(b, i), kv_col)),
            pl.BlockSpec((A_BLOCK, LANES), lambda b, i: (prev(i), 0)),
            pl.BlockSpec((A_BLOCK, LANES), lambda b, i: (prev(i), 0)),
            pl.BlockSpec((A_BLOCK, LANES), lambda b, i: (i, 0)),
            pl.BlockSpec((A_BLOCK, LANES), lambda b, i: (i, 0)),
            pl.BlockSpec((1, LANES), lambda b, i: (0, 0)),
            pl.BlockSpec((1, LANES), lambda b, i: (0, 0)),
            pl.BlockSpec((LANES, LANES), lambda b, i: (0, 0)),
        ],
        out_specs=pl.BlockSpec((A_BLOCK, A_WIDTH), lambda b, i: (row(b, i), 0)),
        out_shape=jax.ShapeDtypeStruct((batch * seq, A_WIDTH), BF16),
        compiler_params=_cparams(("parallel", "parallel")),
        name="swa_mixer",
    )(sinks, proj, proj, proj, proj, cos_t, sin_t, cos_t, sin_t, qn_l, kn_l, grp)


def _hgrn_body(q_ref, f_ref, v_ref, g_ref, lbp_ref, gn_ref, cum_ref, lvl_ref, o_ref, st_ref, *, layer):
    @pl.when(pl.program_id(1) == 0)
    def _():
        st_ref[...] = jnp.zeros_like(st_ref)

    lbp = lbp_ref[...]
    ex = jnp.exp(lbp - jnp.max(lbp, axis=0, keepdims=True))
    sm = ex / jnp.sum(ex, axis=0, keepdims=True)
    lb = jnp.sum(sm[:layer + 1], axis=0, keepdims=True) - sm[0:1]
    lbc = jnp.maximum(lb, LB_FLOOR)
    oml = 1.0 - lb

    s_pos, s_neg = _sigmoid_pair(f_ref[...])
    logf = jnp.log(lbc + oml * s_pos)
    kin = oml * s_neg - (lbc - lb)
    q = _silu(q_ref[...])

    lf_hi, lf_lo = _split_bf16(logf)
    cum = cum_ref[...]
    sums = _dot(cum, lf_hi) + _dot(cum, lf_lo)
    blk = q.shape[0]
    seg = lambda n: sums[n * blk:(n + 1) * blk]
    b_incl, b_rest = seg(2 * B_LEVELS - 2), seg(2 * B_LEVELS - 1)
    b_last = b_incl[blk - 1:blk]

    qs = [(q * jnp.exp(logf)).astype(BF16)]
    ks = [kin.astype(BF16)]
    for lv in range(1, B_LEVELS):
        qs.append((q * jnp.exp(seg(2 * lv - 2))).astype(BF16))
        ks.append((kin * jnp.exp(seg(2 * lv - 1))).astype(BF16))
    qs.append(q.astype(BF16))
    ks.append(ks[0])
    q_state = (q * jnp.exp(b_incl)).astype(BF16)
    k_state = (kin * jnp.exp(b_rest)).astype(BF16)
    decay = jnp.exp(b_last)

    lvl = lvl_ref[...]
    v = v_ref[...]
    for h in range(B_HEADS):
        sl = slice(h * B_HEAD_DIM, (h + 1) * B_HEAD_DIM)
        att = jnp.zeros((blk, blk), F32)
        for n in range(B_LEVELS + 1):
            att = jnp.where(lvl == n, _dot_nt(qs[n][:, sl], ks[n][:, sl]), att)
        vh = v[:, sl]
        st = st_ref[h]
        o = _dot(att.astype(BF16), vh.astype(BF16)) + _dot_nt(q_state[:, sl], st.astype(BF16))
        st_ref[h] = st * decay[:, sl] + _dot(vh.T.astype(BF16), k_state[:, sl])
        ms = jnp.mean(o * o, axis=-1, keepdims=True)
        y = o * lax.rsqrt(ms + EPS) * gn_ref[...] * _silu(g_ref[:, sl])
        o_ref[:, sl] = y.astype(o_ref.dtype)


def _hgrn_constants():
    blk = B_BLOCK
    i = np.arange(blk)[:, None]
    j = np.arange(blk)[None, :]
    mats = []
    lvl = np.full((blk, blk), -1, np.int32)
    lvl[i == j] = B_LEVELS
    for lv in range(B_LEVELS):
        s = 1 << lv
        same = (i // s) == (j // s)
        if lv > 0:
            mats += [same & (j <= i), same & (j > i)]
        lvl[((i // s) % 2 == 1) & ((j // s) == (i // s) - 1)] = lv
    mats += [j <= i, j > i]
    return np.concatenate(mats, axis=0).astype(np.float32), lvl


def _hgrn(proj, lower_bounds, gn, batch, seq, cols, layer):
    nb = seq // B_BLOCK
    cum, lvl = _hgrn_constants()
    row = lambda b, i: b * nb + i
    tile = lambda c: pl.BlockSpec((B_BLOCK, B_WIDTH), lambda b, i: (row(b, i), c))
    whole = lambda a: pl.BlockSpec(a.shape, lambda b, i: (0,) * a.ndim)
    gn2 = gn.reshape(1, B_HEAD_DIM)
    cum = jnp.asarray(cum, BF16)
    lvl = jnp.asarray(lvl)
    return pl.pallas_call(
        functools.partial(_hgrn_body, layer=layer),
        grid=(batch, nb),
        in_specs=[tile(cols[0]), tile(cols[1]), tile(cols[2]), tile(cols[3]),
                  whole(lower_bounds), whole(gn2), whole(cum), whole(lvl)],
        out_specs=pl.BlockSpec((B_BLOCK, B_WIDTH), lambda b, i: (row(b, i), 0)),
        out_shape=jax.ShapeDtypeStruct((batch * seq, B_WIDTH), BF16),
        scratch_shapes=[pltpu.VMEM((B_HEADS, B_HEAD_DIM, B_HEAD_DIM), F32)],
        compiler_params=_cparams(("parallel", "arbitrary")),
        name="hgrn2_mixer",
    )(proj, proj, proj, proj, lower_bounds, gn2, cum, lvl)


def _stick_tile(z2, v_blk, tri, carry, mask):
    e = jnp.exp2(-jnp.abs(z2))
    p = jnp.maximum(z2, 0.0) + jnp.log(1.0 + e) * LOG2E
    log_beta = z2 - p
    if mask is not None:
        p = jnp.where(mask, p, 0.0)
    p_hi, p_lo = _split_bf16(p)
    later = _dot(p_hi, tri) + _dot(p_lo, tri)
    w = jnp.exp2(log_beta - later - carry)
    if mask is not None:
        w = jnp.where(mask, w, 0.0)
    return _dot(w.astype(BF16), v_blk), jnp.sum(p, axis=-1, keepdims=True)


def _stick_body(q_ref, k_ref, v_ref, g_ref, tri_ref, o_ref):
    i = pl.program_id(2)
    t = q_ref.shape[0]
    q = q_ref[...]
    tri = tri_ref[...]
    row = lax.broadcasted_iota(jnp.int32, (t, t), 0)
    col = lax.broadcasted_iota(jnp.int32, (t, t), 1)

    def tile(kb, carry, mask):
        start = pl.multiple_of(kb * t, t)
        z2 = _dot_nt(q, k_ref[pl.ds(start, t), :])
        return _stick_tile(z2, v_ref[pl.ds(start, t), :], tri, carry, mask)

    acc, carry = tile(i, jnp.zeros((t, 1), F32), col < row)

    def more(state):
        n, _, _, min_carry = state
        return (n < i) & (min_carry < C_DONE_BITS)

    def step(state):
        n, acc, carry, _ = state
        o, tot = tile(i - 1 - n, carry, None)
        carry = carry + tot
        return n + 1, acc + o, carry, jnp.min(carry)

    _, acc, _, _ = lax.while_loop(more, step, (jnp.int32(0), acc, carry, jnp.float32(0.0)))
    o_ref[...] = (acc * _silu(g_ref[...])).astype(o_ref.dtype)


def _stick(qkv, gate, batch, seq):
    t = C_TILE
    nq = seq // t
    tri = jnp.asarray(np.arange(t)[:, None] > np.arange(t)[None, :], BF16)
    row = lambda b, h, i: b * nq + i
    return pl.pallas_call(
        _stick_body,
        grid=(batch, C_HEADS, nq),
        in_specs=[
            pl.BlockSpec((t, C_HEAD_DIM), lambda b, h, i: (row(b, h, i), h)),
            pl.BlockSpec((seq, C_HEAD_DIM), lambda b, h, i: (b, C_HEADS + h)),
            pl.BlockSpec((seq, C_HEAD_DIM), lambda b, h, i: (b, 2 * C_HEADS + h)),
            pl.BlockSpec((t, C_HEAD_DIM), lambda b, h, i: (row(b, h, i), h)),
            pl.BlockSpec((t, t), lambda b, h, i: (0, 0)),
        ],
        out_specs=pl.BlockSpec((t, C_HEAD_DIM), lambda b, h, i: (row(b, h, i), h)),
        out_shape=jax.ShapeDtypeStruct((batch * seq, C_WIDTH), BF16),
        compiler_params=_cparams(("parallel", "parallel", "arbitrary")),
        name="stick_breaking_mixer",
    )(qkv, qkv, qkv, gate, tri)


def _even_layout():
    lane = np.arange(LANES)
    dim = lane % A_HALF + A_HALF * (lane // (2 * A_HALF))
    aq = np.concatenate([(2 * p + (lane // A_HALF) % 2) * A_HEAD_DIM + dim for p in range(A_PAIRS)])
    kv0 = A_WIDTH
    ak = np.concatenate([kv0 + g * A_HEAD_DIM + dim for g in range(A_KV_HEADS)])
    v0 = kv0 + A_KV_HEADS * A_HEAD_DIM
    av = np.concatenate([v0 + g * A_HEAD_DIM + lane % A_HEAD_DIM for g in range(A_KV_HEADS)])
    rest0 = v0 + A_KV_HEADS * A_HEAD_DIM
    rest = rest0 + np.arange(A_WIDTH + 4 * B_WIDTH)
    cols = np.concatenate([aq, rest, ak, av])
    grp = ((lane[:, None] // A_HALF) % 2 == (lane[None, :] // A_HALF) % 2).astype(np.float32)
    return cols, dim, grp


def _rope_tables(seq):
    half = A_HALF
    inv_freq = 1.0 / (ROPE_THETA ** (jnp.arange(half, dtype=F32) / half))
    ang = jnp.arange(seq, dtype=F32)[:, None] * inv_freq[None, :]
    cos, sin = jnp.cos(ang), jnp.sin(ang)
    return jnp.tile(cos, (1, 4)), jnp.concatenate([-sin, -sin, sin, sin], axis=1)


def _even_layer(x2, batch, seq, ln, w_in, qn, kn, sinks, lower_bounds, gn, w_out, layer, tables):
    cols, dim, grp = _even_layout()
    w = w_in[:, cols].astype(BF16)
    n = w.shape[1]
    tn = n // 4
    proj, = _inproj(x2, ln, w, jnp.ones((1, n), F32), [(n, F32)], tm=512, tn=tn)
    qn_l = (qn[dim] * A_HEAD_DIM ** -0.5).reshape(1, LANES)
    kn_l = kn[dim].reshape(1, LANES)
    kv_col = (2 * A_WIDTH + 4 * B_WIDTH) // (2 * A_KV_HEADS * LANES)
    ya = _swa(proj, sinks, tables[0], tables[1], qn_l, kn_l, jnp.asarray(grp, BF16),
              batch, seq, (0, 1, kv_col))
    yb = _hgrn(proj, lower_bounds, gn, batch, seq, (2, 3, 4, 5), layer)
    return _outproj(ya, yb, 0, 0, w_out.astype(BF16), x2, tm=512)


def _odd_layer(x2, batch, seq, ln, w_in, w_out):
    n = w_in.shape[1]
    qscale = C_HEAD_DIM ** -0.5 * LOG2E
    colscale = jnp.concatenate([jnp.full((1, C_WIDTH), qscale, F32), jnp.ones((1, n - C_WIDTH), F32)], axis=1)
    qkv, gate = _inproj(x2, ln, w_in.astype(BF16), colscale,
                        [(3 * C_WIDTH, BF16), (C_WIDTH, F32)], tm=512, tn=C_WIDTH)
    y = _stick(qkv, gate, batch, seq)
    return _outproj(y, y, 0, 1, w_out.astype(BF16), x2, tm=512)


def kernel(x, ln_even, w_in_even, q_norm_a, k_norm_a, sinks_a, lower_bounds, g_norm_b, w_out_even,
           ln_odd, w_in_odd, w_out_odd):
    batch, seq, d = x.shape
    depth = ln_even.shape[0] + ln_odd.shape[0]
    x2 = x.reshape(batch * seq, d)
    tables = _rope_tables(seq)
    for layer in range(depth):
        e = layer // 2
        if layer % 2 == 0:
            x2 = _even_layer(x2, batch, seq, ln_even[e], w_in_even[e], q_norm_a[e], k_norm_a[e],
                             sinks_a[e], lower_bounds, g_norm_b[e], w_out_even[e], e, tables)
        else:
            x2 = _odd_layer(x2, batch, seq, ln_odd[e], w_in_odd[e], w_out_odd[e])
    return x2.reshape(batch, seq, d)
```

```python
import functools
import math

import jax
import jax.numpy as jnp
import numpy as np
from jax import lax
from jax.experimental import pallas as pl
from jax.experimental.pallas import tpu as pltpu

F32 = jnp.float32
BF16 = jnp.bfloat16

EPS = 1e-6
ROPE_THETA = 10000.0
LB_FLOOR = 1e-30
NEG_BIG = -1e30

V7X_VMEM_BYTES = 64 * 1024 * 1024
VMEM_LIMIT = V7X_VMEM_BYTES - 8 * 1024 * 1024
LANES = 128

A_HEAD_DIM = 64
A_HALF = A_HEAD_DIM // 2
A_HEADS = 16
A_KV_HEADS = 2
A_GROUP = A_HEADS // A_KV_HEADS
A_PAIRS = A_HEADS // 2
A_WIDTH = A_HEADS * A_HEAD_DIM
A_BLOCK = 128
B_HEAD_DIM = 128
B_HEADS = 8
B_WIDTH = B_HEADS * B_HEAD_DIM
B_BLOCK = 128
B_LEVELS = int(math.log2(B_BLOCK))
C_HEAD_DIM = 128
C_HEADS = 16
C_WIDTH = C_HEADS * C_HEAD_DIM
C_TILE = 256
C_HEADS_PER_STEP = 4
LOG2E = 1.0 / math.log(2.0)
C_DONE_BITS = 150.0
C_NO_TILE = 1e30


def _cparams(semantics):
    return pltpu.CompilerParams(dimension_semantics=semantics, vmem_limit_bytes=VMEM_LIMIT)


def _split_bf16(x):
    hi = x.astype(BF16)
    lo = (x - hi.astype(F32)).astype(BF16)
    return hi, lo


def _dot(a, b):
    return jnp.dot(a, b, preferred_element_type=F32)


def _dot_nt(a, b):
    return lax.dot_general(a, b, (((1,), (1,)), ((), ())), preferred_element_type=F32)


def _sigmoid_pair(x):
    half_t = 0.5 * jnp.tanh(0.5 * x)
    return 0.5 + half_t, 0.5 - half_t


def _silu(x):
    return x * _sigmoid_pair(x)[0]


def _inproj_body(x_ref, ln_ref, w_ref, *rest, seg_tiles, scaled_tiles, scale):
    out_refs, h_ref = rest[:len(seg_tiles)], rest[len(seg_tiles)]
    j = pl.program_id(1)

    @pl.when(j == 0)
    def _():
        x = x_ref[...]
        ms = jnp.mean(x * x, axis=-1, keepdims=True)
        h_ref[...] = (x * lax.rsqrt(ms + EPS) * ln_ref[...]).astype(BF16)

    acc = _dot(h_ref[...], w_ref[...])
    if scaled_tiles:
        acc = acc * jnp.where(j < scaled_tiles, scale, 1.0)
    start = 0
    for o_ref, nt in zip(out_refs, seg_tiles):
        @pl.when((j >= start) & (j < start + nt))
        def _(o_ref=o_ref):
            o_ref[...] = acc.astype(o_ref.dtype)
        start += nt


def _inproj(x2, ln, w_bf16, segs, tm, tn, scaled_cols=0, scale=1.0):
    t, d = x2.shape
    n = w_bf16.shape[1]
    seg_tiles = tuple(wd // tn for wd, _ in segs)
    assert sum(seg_tiles) * tn == n and t % tm == 0 and scaled_cols % tn == 0
    out_specs, out_shapes, start = [], [], 0
    for (wd, dt), nt in zip(segs, seg_tiles):
        out_specs.append(pl.BlockSpec(
            (tm, tn), lambda i, j, s=start, m=nt - 1: (i, jnp.clip(j - s, 0, m))))
        out_shapes.append(jax.ShapeDtypeStruct((t, wd), dt))
        start += nt
    return pl.pallas_call(
        functools.partial(_inproj_body, seg_tiles=seg_tiles, scaled_tiles=scaled_cols // tn, scale=scale),
        grid=(t // tm, n // tn),
        in_specs=[
            pl.BlockSpec((tm, d), lambda i, j: (i, 0)),
            pl.BlockSpec((1, d), lambda i, j: (0, 0)),
            pl.BlockSpec((d, tn), lambda i, j: (0, j)),
        ],
        out_specs=out_specs,
        out_shape=out_shapes,
        scratch_shapes=[pltpu.VMEM((tm, d), BF16)],
        compiler_params=_cparams(("parallel", "arbitrary")),
        name="norm_inproj",
    )(x2, ln.reshape(1, d), w_bf16)


def _outproj_body(ya_ref, yb_ref, wa_ref, wb_ref, x_ref, o_ref):
    acc = _dot(ya_ref[...], wa_ref[...]) + _dot(yb_ref[...], wb_ref[...])
    o_ref[...] = x_ref[...] + acc


def _outproj(ya, yb, a_col, b_col, w_bf16, x2, tm):
    t, d = x2.shape
    half = w_bf16.shape[0] // 2
    return pl.pallas_call(
        _outproj_body,
        grid=(t // tm,),
        in_specs=[
            pl.BlockSpec((tm, half), lambda i: (i, a_col)),
            pl.BlockSpec((tm, half), lambda i: (i, b_col)),
            pl.BlockSpec((half, d), lambda i: (0, 0)),
            pl.BlockSpec((half, d), lambda i: (1, 0)),
            pl.BlockSpec((tm, d), lambda i: (i, 0)),
        ],
        out_specs=pl.BlockSpec((tm, d), lambda i: (i, 0)),
        out_shape=jax.ShapeDtypeStruct((t, d), F32),
        compiler_params=_cparams(("parallel",)),
        name="outproj_residual",
    )(ya, yb, w_bf16, w_bf16, x2)


def _swa_body(sink_ref, q_ref, g_ref, kvp_ref, kvc_ref, cosp_ref, sinp_ref, cosc_ref, sinc_ref,
              qn_ref, kn_ref, grp_ref, o_ref):
    i = pl.program_id(1)
    blk = q_ref.shape[0]
    lane = lax.broadcasted_iota(jnp.int32, (blk, LANES), 1)
    row = lax.broadcasted_iota(jnp.int32, (blk, 2 * blk), 0)
    col = lax.broadcasted_iota(jnp.int32, (blk, 2 * blk), 1)
    no_prev = jnp.where(i > 0, 0, 2 * blk)
    valid = ((col < blk) & (col > row + no_prev)) | ((col >= blk) & (col - blk <= row))
    cos_c, sin_c = cosc_ref[...], sinc_ref[...]

    def rope(x, cos, sin):
        return x * cos + pltpu.roll(x, 2 * A_HALF, axis=1) * sin

    def keys(kv_ref, cos, sin, g):
        k = kv_ref[:, g * LANES:(g + 1) * LANES]
        ms = jnp.sum(k * k, axis=-1, keepdims=True) * (1.0 / LANES)
        k = rope(k * lax.rsqrt(ms + EPS) * kn_ref[...], cos, sin)
        v = kv_ref[:, (A_KV_HEADS + g) * LANES:(A_KV_HEADS + g + 1) * LANES]
        return k.astype(BF16), v.astype(BF16)

    band = []
    for g in range(A_KV_HEADS):
        kp, vp = keys(kvp_ref, cosp_ref[...], sinp_ref[...], g)
        kc, vc = keys(kvc_ref, cos_c, sin_c, g)
        band.append((jnp.concatenate([kp, kc], axis=0), jnp.concatenate([vp, vc], axis=0)))

    grp = grp_ref[...]
    for p in range(A_PAIRS):
        kb, vb = band[(2 * p) // A_GROUP]
        q = q_ref[:, p * LANES:(p + 1) * LANES]
        sq_hi, sq_lo = _split_bf16(q * q)
        ms = (_dot(sq_hi, grp) + _dot(sq_lo, grp)) * (1.0 / A_HEAD_DIM)
        q = rope(q * lax.rsqrt(ms + EPS) * qn_ref[...], cos_c, sin_c)
        outs = []
        for hh in range(2):
            qh = jnp.where((lane // A_HALF) % 2 == hh, q, 0.0).astype(BF16)
            s = jnp.where(valid, _dot_nt(qh, kb), NEG_BIG)
            sink = sink_ref[2 * p + hh]
            m = jnp.maximum(jnp.max(s, axis=-1, keepdims=True), sink)
            e = jnp.exp(s - m)
            denom = jnp.sum(e, axis=-1, keepdims=True) + jnp.exp(sink - m)
            outs.append(_dot(e.astype(BF16), vb) / denom)
        o = jnp.where(lane < A_HEAD_DIM, outs[0], outs[1])
        o_ref[:, p * LANES:(p + 1) * LANES] = (o * _silu(g_ref[:, p * LANES:(p + 1) * LANES])).astype(o_ref.dtype)


def _swa(proj, sinks, cos_t, sin_t, qn_l, kn_l, grp, batch, seq, cols):
    nb = seq // A_BLOCK
    q_col, g_col, kv_col = cols
    kv_w = 2 * A_KV_HEADS * LANES
    row = lambda b, i: b * nb + i
    prev = lambda i: jnp.maximum(i - 1, 0)
    return pl.pallas_call(
        _swa_body,
        grid=(batch, nb),
        in_specs=[
            pl.BlockSpec(memory_space=pltpu.SMEM),
            pl.BlockSpec((A_BLOCK, A_WIDTH), lambda b, i: (row(b, i), q_col)),
            pl.BlockSpec((A_BLOCK, A_WIDTH), lambda b, i: (row(b, i), g_col)),
            pl.BlockSpec((A_BLOCK, kv_w), lambda b, i: (row(b, prev(i)), kv_col)),
            pl.BlockSpec((A_BLOCK, kv_w), lambda b, i: (row(b, i), kv_col)),
            pl.BlockSpec((A_BLOCK, LANES), lambda b, i: (prev(i), 0)),
            pl.BlockSpec((A_BLOCK, LANES), lambda b, i: (prev(i), 0)),
            pl.BlockSpec((A_BLOCK, LANES), lambda b, i: (i, 0)),
            pl.BlockSpec((A_BLOCK, LANES), lambda b, i: (i, 0)),
            pl.BlockSpec((1, LANES), lambda b, i: (0, 0)),
            pl.BlockSpec((1, LANES), lambda b, i: (0, 0)),
            pl.BlockSpec((LANES, LANES), lambda b, i: (0, 0)),
        ],
        out_specs=pl.BlockSpec((A_BLOCK, A_WIDTH), lambda b, i: (row(b, i), 0)),
        out_shape=jax.ShapeDtypeStruct((batch * seq, A_WIDTH), BF16),
        compiler_params=_cparams(("parallel", "parallel")),
        name="swa_mixer",
    )(sinks, proj, proj, proj, proj, cos_t, sin_t, cos_t, sin_t, qn_l, kn_l, grp)


def _hgrn_body(q_ref, f_ref, v_ref, g_ref, lbp_ref, gn_ref, cum_ref, lvl_ref, o_ref, st_ref, *, layer):
    @pl.when(pl.program_id(1) == 0)
    def _():
        st_ref[...] = jnp.zeros_like(st_ref)

    lbp = lbp_ref[...]
    ex = jnp.exp(lbp - jnp.max(lbp, axis=0, keepdims=True))
    sm = ex / jnp.sum(ex, axis=0, keepdims=True)
    lb = jnp.sum(sm[:layer + 1], axis=0, keepdims=True) - sm[0:1]
    lbc = jnp.maximum(lb, LB_FLOOR)
    oml = 1.0 - lb

    s_pos, s_neg = _sigmoid_pair(f_ref[...])
    logf = jnp.log(lbc + oml * s_pos)
    kin = oml * s_neg - (lbc - lb)
    q = _silu(q_ref[...])

    sums = _dot(cum_ref[...], jnp.concatenate(_split_bf16(logf), axis=0))
    blk = q.shape[0]
    seg = lambda n: sums[n * blk:(n + 1) * blk]
    b_incl, b_rest = seg(B_LEVELS), seg(B_LEVELS + 1)
    b_last = b_incl[blk - 1:blk]

    qs, ks = [], []
    for lv in range(B_LEVELS):
        decay_lv = jnp.exp(seg(lv))
        qs.append((q * decay_lv).astype(BF16))
        ks.append((kin * decay_lv).astype(BF16))
    qs.append(q.astype(BF16))
    ks.append(kin.astype(BF16))
    q_state = (q * jnp.exp(b_incl)).astype(BF16)
    k_state = (kin * jnp.exp(b_rest)).astype(BF16)
    decay = jnp.exp(b_last)

    lvl = lvl_ref[...]
    at_level = [lvl == n for n in range(B_LEVELS + 1)]
    v = v_ref[...]
    for h in range(B_HEADS):
        sl = slice(h * B_HEAD_DIM, (h + 1) * B_HEAD_DIM)
        att = jnp.zeros((blk, blk), F32)
        for n in range(B_LEVELS + 1):
            att = jnp.where(at_level[n], _dot_nt(qs[n][:, sl], ks[n][:, sl]), att)
        vh = v[:, sl]
        st = st_ref[h]
        o = _dot(att.astype(BF16), vh.astype(BF16)) + _dot_nt(q_state[:, sl], st.astype(BF16))
        st_ref[h] = st * decay[:, sl] + _dot(vh.T.astype(BF16), k_state[:, sl])
        ms = jnp.mean(o * o, axis=-1, keepdims=True)
        y = o * lax.rsqrt(ms + EPS) * gn_ref[...] * _silu(g_ref[:, sl])
        o_ref[:, sl] = y.astype(o_ref.dtype)


def _hgrn_constants():
    blk = B_BLOCK
    i = np.arange(blk)[:, None]
    j = np.arange(blk)[None, :]
    mats = []
    lvl = np.full((blk, blk), -1, np.int32)
    lvl[i == j] = B_LEVELS
    for lv in range(B_LEVELS):
        s = 1 << lv
        same = (i // s) == (j // s)
        odd = (i // s) % 2 == 1
        mats.append(same & np.where(odd, j <= i, j > i))
        lvl[odd & ((j // s) == (i // s) - 1)] = lv
    mats += [j <= i, j > i]
    cum = np.concatenate(mats, axis=0).astype(np.float32)
    return np.concatenate([cum, cum], axis=1), lvl


def _hgrn(proj, lower_bounds, gn, batch, seq, cols, layer):
    nb = seq // B_BLOCK
    cum, lvl = _hgrn_constants()
    row = lambda b, i: b * nb + i
    tile = lambda c: pl.BlockSpec((B_BLOCK, B_WIDTH), lambda b, i: (row(b, i), c))
    whole = lambda a: pl.BlockSpec(a.shape, lambda b, i: (0,) * a.ndim)
    gn2 = gn.reshape(1, B_HEAD_DIM)
    cum = jnp.asarray(cum, BF16)
    lvl = jnp.asarray(lvl)
    return pl.pallas_call(
        functools.partial(_hgrn_body, layer=layer),
        grid=(batch, nb),
        in_specs=[tile(cols[0]), tile(cols[1]), tile(cols[2]), tile(cols[3]),
                  whole(lower_bounds), whole(gn2), whole(cum), whole(lvl)],
        out_specs=pl.BlockSpec((B_BLOCK, B_WIDTH), lambda b, i: (row(b, i), 0)),
        out_shape=jax.ShapeDtypeStruct((batch * seq, B_WIDTH), BF16),
        scratch_shapes=[pltpu.VMEM((B_HEADS, B_HEAD_DIM, B_HEAD_DIM), F32)],
        compiler_params=_cparams(("parallel", "arbitrary")),
        name="hgrn2_mixer",
    )(proj, proj, proj, proj, lower_bounds, gn2, cum, lvl)


def _stick_tiles(qs, ks, vs, masks, tri2, carry_of):
    t = qs[0].shape[0]
    z2s, lhs = [], []
    for q, k_blk, mask in zip(qs, ks, masks):
        z2 = _dot_nt(q, k_blk)
        p = jnp.maximum(z2, 0.0) + jnp.log(1.0 + jnp.exp2(-jnp.abs(z2))) * LOG2E
        if mask is not None:
            p = jnp.where(mask, p, 0.0)
        z2s.append(z2)
        lhs.append(jnp.concatenate(_split_bf16(p), axis=1))
    incl = _dot(jnp.concatenate(lhs, axis=0), tri2)
    incls = [incl[n * t:(n + 1) * t] for n in range(len(qs))]
    totals = [x[:, 0:1] for x in incls]
    outs = []
    for n, (z2, v_blk, mask) in enumerate(zip(z2s, vs, masks)):
        w = jnp.exp2(z2 - incls[n] - carry_of(n, totals))
        if mask is not None:
            w = jnp.where(mask, w, 0.0)
        outs.append(_dot(w.astype(BF16), v_blk))
    return outs, totals


def _stick_body(q_ref, k_ref, v_ref, g_ref, tri_ref, o_ref):
    i = pl.program_id(2)
    t = q_ref.shape[0]
    heads = q_ref.shape[1] // C_HEAD_DIM
    hs = [slice(h * C_HEAD_DIM, (h + 1) * C_HEAD_DIM) for h in range(heads)]
    qs = [q_ref[:, s] for s in hs]
    tri2 = tri_ref[...]
    row = lax.broadcasted_iota(jnp.int32, (t, t), 0)
    col = lax.broadcasted_iota(jnp.int32, (t, t), 1)

    def key_tiles(kb):
        start = pl.multiple_of(kb * t, t)
        return ([k_ref[pl.ds(start, t), s] for s in hs], [v_ref[pl.ds(start, t), s] for s in hs])

    kd, vd = key_tiles(i)
    kp, vp = key_tiles(jnp.maximum(i - 1, 0))
    no_prev = jnp.where(i > 0, 0.0, C_NO_TILE)
    outs, totals = _stick_tiles(
        qs + qs, kd + kp, vd + vp, [col < row] * heads + [None] * heads, tri2,
        lambda n, tot: 0.0 if n < heads else tot[n - heads] + no_prev)
    accs = tuple(outs[h] + outs[heads + h] for h in range(heads))
    carries = tuple(totals[h] + totals[heads + h] for h in range(heads))

    def lowest(carries):
        return functools.reduce(jnp.minimum, [jnp.min(c) for c in carries])

    def more(state):
        n, _, _, min_carry = state
        return (n < i) & (min_carry < C_DONE_BITS)

    def step(state):
        n, accs, carries, _ = state
        kn, vn = key_tiles(i - 1 - n)
        outs, totals = _stick_tiles(qs, kn, vn, [None] * heads, tri2, lambda h, tot: carries[h])
        accs = tuple(a + o for a, o in zip(accs, outs))
        carries = tuple(c + tot for c, tot in zip(carries, totals))
        return n + 1, accs, carries, lowest(carries)

    _, accs, _, _ = lax.while_loop(more, step, (jnp.int32(1), accs, carries, lowest(carries)))
    for h in range(heads):
        o_ref[:, hs[h]] = (accs[h] * _silu(g_ref[:, hs[h]])).astype(o_ref.dtype)


def _stick(qkv, gate, batch, seq):
    t = C_TILE
    nq = seq // t
    hw = C_HEADS_PER_STEP * C_HEAD_DIM
    groups = C_HEADS // C_HEADS_PER_STEP
    tri = (np.arange(t)[:, None] >= np.arange(t)[None, :]).astype(np.float32)
    tri2 = jnp.asarray(np.concatenate([tri, tri], axis=0), BF16)
    row = lambda b, h, i: b * nq + i
    return pl.pallas_call(
        _stick_body,
        grid=(batch, groups, nq),
        in_specs=[
            pl.BlockSpec((t, hw), lambda b, h, i: (row(b, h, i), h)),
            pl.BlockSpec((seq, hw), lambda b, h, i: (b, groups + h)),
            pl.BlockSpec((seq, hw), lambda b, h, i: (b, 2 * groups + h)),
            pl.BlockSpec((t, hw), lambda b, h, i: (row(b, h, i), h)),
            pl.BlockSpec((2 * t, t), lambda b, h, i: (0, 0)),
        ],
        out_specs=pl.BlockSpec((t, hw), lambda b, h, i: (row(b, h, i), h)),
        out_shape=jax.ShapeDtypeStruct((batch * seq, C_WIDTH), BF16),
        compiler_params=_cparams(("parallel", "parallel", "arbitrary")),
        name="stick_breaking_mixer",
    )(qkv, qkv, qkv, gate, tri2)


def _even_layout():
    lane = np.arange(LANES)
    dim = lane % A_HALF + A_HALF * (lane // (2 * A_HALF))
    aq = np.concatenate([(2 * p + (lane // A_HALF) % 2) * A_HEAD_DIM + dim for p in range(A_PAIRS)])
    kv0 = A_WIDTH
    ak = np.concatenate([kv0 + g * A_HEAD_DIM + dim for g in range(A_KV_HEADS)])
    v0 = kv0 + A_KV_HEADS * A_HEAD_DIM
    av = np.concatenate([v0 + g * A_HEAD_DIM + lane % A_HEAD_DIM for g in range(A_KV_HEADS)])
    rest0 = v0 + A_KV_HEADS * A_HEAD_DIM
    rest = rest0 + np.arange(A_WIDTH + 4 * B_WIDTH)
    cols = np.concatenate([aq, rest, ak, av])
    grp = ((lane[:, None] // A_HALF) % 2 == (lane[None, :] // A_HALF) % 2).astype(np.float32)
    return cols, dim, grp


def _rope_tables(seq):
    half = A_HALF
    inv_freq = 1.0 / (ROPE_THETA ** (jnp.arange(half, dtype=F32) / half))
    ang = jnp.arange(seq, dtype=F32)[:, None] * inv_freq[None, :]
    cos, sin = jnp.cos(ang), jnp.sin(ang)
    return jnp.tile(cos, (1, 4)), jnp.concatenate([-sin, -sin, sin, sin], axis=1)


def _even_layer(x2, batch, seq, ln, w_in, qn, kn, sinks, lower_bounds, gn, w_out, layer, tables):
    cols, dim, grp = _even_layout()
    w = w_in[:, cols].astype(BF16)
    n = w.shape[1]
    proj, = _inproj(x2, ln, w, [(n, F32)], tm=1024, tn=n // 4)
    qn_l = (qn[dim] * A_HEAD_DIM ** -0.5).reshape(1, LANES)
    kn_l = kn[dim].reshape(1, LANES)
    kv_col = (2 * A_WIDTH + 4 * B_WIDTH) // (2 * A_KV_HEADS * LANES)
    ya = _swa(proj, sinks, tables[0], tables[1], qn_l, kn_l, jnp.asarray(grp, BF16),
              batch, seq, (0, 1, kv_col))
    yb = _hgrn(proj, lower_bounds, gn, batch, seq, (2, 3, 4, 5), layer)
    return _outproj(ya, yb, 0, 0, w_out.astype(BF16), x2, tm=512)


def _odd_layer(x2, batch, seq, ln, w_in, w_out):
    qkv, gate = _inproj(x2, ln, w_in.astype(BF16), [(3 * C_WIDTH, BF16), (C_WIDTH, F32)],
                        tm=1024, tn=C_WIDTH // 2, scaled_cols=C_WIDTH, scale=C_HEAD_DIM ** -0.5 * LOG2E)
    y = _stick(qkv, gate, batch, seq)
    return _outproj(y, y, 0, 1, w_out.astype(BF16), x2, tm=512)


def kernel(x, ln_even, w_in_even, q_norm_a, k_norm_a, sinks_a, lower_bounds, g_norm_b, w_out_even,
           ln_odd, w_in_odd, w_out_odd):
    batch, seq, d = x.shape
    depth = ln_even.shape[0] + ln_odd.shape[0]
    x2 = x.reshape(batch * seq, d)
    tables = _rope_tables(seq)
    for layer in range(depth):
        e = layer // 2
        if layer % 2 == 0:
            x2 = _even_layer(x2, batch, seq, ln_even[e], w_in_even[e], q_norm_a[e], k_norm_a[e],
                             sinks_a[e], lower_bounds, g_norm_b[e], w_out_even[e], e, tables)
        else:
            x2 = _odd_layer(x2, batch, seq, ln_odd[e], w_in_odd[e], w_out_odd[e])
    return x2.reshape(batch, seq, d)
```

```python
import functools
import math

import jax
import jax.numpy as jnp
import numpy as np
from jax import lax
from jax.experimental import pallas as pl
from jax.experimental.pallas import tpu as pltpu

F32 = jnp.float32
BF16 = jnp.bfloat16

EPS = 1e-6
ROPE_THETA = 10000.0
LB_FLOOR = 1e-30
NEG_BIG = -1e30

V7X_VMEM_BYTES = 64 * 1024 * 1024
VMEM_LIMIT = V7X_VMEM_BYTES - 8 * 1024 * 1024
LANES = 128
SUBLANES = 8

A_HEAD_DIM = 64
A_HALF = A_HEAD_DIM // 2
A_HEADS = 16
A_KV_HEADS = 2
A_GROUP = A_HEADS // A_KV_HEADS
A_PAIRS = A_HEADS // 2
A_WIDTH = A_HEADS * A_HEAD_DIM
A_BLOCK = 128
B_HEAD_DIM = 128
B_HEADS = 8
B_WIDTH = B_HEADS * B_HEAD_DIM
B_BLOCK = 128
B_LEVELS = int(math.log2(B_BLOCK))
C_HEAD_DIM = 128
C_HEADS = 16
C_WIDTH = C_HEADS * C_HEAD_DIM
C_TILE = 256
C_HEADS_PER_STEP = 4
LOG2E = 1.0 / math.log(2.0)
C_DONE_BITS = 150.0
C_NO_TILE = 1e30


def _cparams(semantics):
    return pltpu.CompilerParams(dimension_semantics=semantics, vmem_limit_bytes=VMEM_LIMIT)


def _split_bf16(x):
    hi = x.astype(BF16)
    lo = (x - hi.astype(F32)).astype(BF16)
    return hi, lo


def _dot(a, b):
    return jnp.dot(a, b, preferred_element_type=F32)


def _dot_nt(a, b):
    return lax.dot_general(a, b, (((1,), (1,)), ((), ())), preferred_element_type=F32)


def _sigmoid_pair(x):
    half_t = 0.5 * jnp.tanh(0.5 * x)
    return 0.5 + half_t, 0.5 - half_t


def _silu(x):
    return x * _sigmoid_pair(x)[0]


def _rmsnorm_bf16(x, gain):
    ms = jnp.mean(x * x, axis=-1, keepdims=True)
    return (x * lax.rsqrt(ms + EPS) * gain).astype(BF16)


def _norm_body(x_ref, ln_ref, h_ref):
    h_ref[...] = _rmsnorm_bf16(x_ref[...], ln_ref[...])


def _norm(x2, ln, tm):
    t, d = x2.shape
    return pl.pallas_call(
        _norm_body,
        grid=(t // tm,),
        in_specs=[pl.BlockSpec((tm, d), lambda i: (i, 0)), pl.BlockSpec((1, d), lambda i: (0, 0))],
        out_specs=pl.BlockSpec((tm, d), lambda i: (i, 0)),
        out_shape=jax.ShapeDtypeStruct((t, d), BF16),
        compiler_params=_cparams(("parallel",)),
        name="rmsnorm",
    )(x2, ln.reshape(1, d))


def _inproj_body(h_ref, w_ref, *out_refs, seg_tiles, scaled_tiles, scale):
    j = pl.program_id(1)
    acc = _dot(h_ref[...], w_ref[...])
    if scaled_tiles:
        acc = acc * jnp.where(j < scaled_tiles, scale, 1.0)
    start = 0
    for o_ref, nt in zip(out_refs, seg_tiles):
        @pl.when((j >= start) & (j < start + nt))
        def _(o_ref=o_ref):
            o_ref[...] = acc.astype(o_ref.dtype)
        start += nt


def _inproj(h, w_bf16, layer, segs, tm, tn, scaled_cols=0, scale=1.0):
    t, d = h.shape
    n = w_bf16.shape[2]
    seg_tiles = tuple(wd // tn for wd, _ in segs)
    assert sum(seg_tiles) * tn == n and t % tm == 0 and scaled_cols % tn == 0
    out_specs, out_shapes, start = [], [], 0
    for (wd, dt), nt in zip(segs, seg_tiles):
        out_specs.append(pl.BlockSpec(
            (tm, tn), lambda i, j, s=start, m=nt - 1: (i, jnp.clip(j - s, 0, m))))
        out_shapes.append(jax.ShapeDtypeStruct((t, wd), dt))
        start += nt
    return pl.pallas_call(
        functools.partial(_inproj_body, seg_tiles=seg_tiles, scaled_tiles=scaled_cols // tn, scale=scale),
        grid=(t // tm, n // tn),
        in_specs=[
            pl.BlockSpec((tm, d), lambda i, j: (i, 0)),
            pl.BlockSpec((None, d, tn), lambda i, j: (layer, 0, j)),
        ],
        out_specs=out_specs,
        out_shape=out_shapes,
        compiler_params=_cparams(("parallel", "arbitrary")),
        name="inproj",
    )(h, w_bf16)


def _outproj_body(ya_ref, yb_ref, wa_ref, wb_ref, x_ref, *rest):
    x = x_ref[...] + _dot(ya_ref[...], wa_ref[...]) + _dot(yb_ref[...], wb_ref[...])
    if len(rest) == 1:
        rest[0][...] = x
    else:
        ln_ref, o_ref, h_ref = rest
        o_ref[...] = x
        h_ref[...] = _rmsnorm_bf16(x, ln_ref[...])


def _outproj(ya, yb, a_col, b_col, w_bf16, layer, x2, next_ln, tm):
    t, d = x2.shape
    half = w_bf16.shape[1] // 2
    row_tile = pl.BlockSpec((tm, d), lambda i: (i, 0))
    in_specs = [
        pl.BlockSpec((tm, half), lambda i: (i, a_col)),
        pl.BlockSpec((tm, half), lambda i: (i, b_col)),
        pl.BlockSpec((None, half, d), lambda i: (layer, 0, 0)),
        pl.BlockSpec((None, half, d), lambda i: (layer, 1, 0)),
        row_tile,
    ]
    args = [ya, yb, w_bf16, w_bf16, x2]
    out_specs, out_shape = row_tile, jax.ShapeDtypeStruct((t, d), F32)
    if next_ln is not None:
        in_specs.append(pl.BlockSpec((1, d), lambda i: (0, 0)))
        args.append(next_ln.reshape(1, d))
        out_specs, out_shape = [row_tile, row_tile], [out_shape, jax.ShapeDtypeStruct((t, d), BF16)]
    out = pl.pallas_call(
        _outproj_body,
        grid=(t // tm,),
        in_specs=in_specs,
        out_specs=out_specs,
        out_shape=out_shape,
        compiler_params=_cparams(("parallel",)),
        name="outproj_residual",
    )(*args)
    return (out, None) if next_ln is None else tuple(out)


def _swa_body(sink_ref, q_ref, g_ref, kvp_ref, kvc_ref, cosp_ref, sinp_ref, cosc_ref, sinc_ref,
              qn_ref, kn_ref, grp_ref, o_ref):
    i = pl.program_id(1)
    blk = q_ref.shape[0]
    lane = lax.broadcasted_iota(jnp.int32, (blk, LANES), 1)
    first_half = (lane // A_HALF) % 2 == 0
    low_head = lane < A_HEAD_DIM
    row = lax.broadcasted_iota(jnp.int32, (blk, 2 * blk), 0)
    col = lax.broadcasted_iota(jnp.int32, (blk, 2 * blk), 1)
    no_prev = jnp.where(i > 0, 0, 2 * blk)
    valid = ((col < blk) & (col > row + no_prev)) | ((col >= blk) & (col - blk <= row))
    cos_c, sin_c = cosc_ref[...], sinc_ref[...]
    grp = grp_ref[...]

    def norm_rope(x, gain, cos, sin):
        ms = _dot(jnp.concatenate(_split_bf16(x * x), axis=1), grp) * (1.0 / A_HEAD_DIM)
        x = x * lax.rsqrt(ms + EPS) * gain
        partner = jnp.where(first_half, pltpu.roll(x, LANES - A_HALF, axis=1), pltpu.roll(x, A_HALF, axis=1))
        return x * cos + partner * sin

    def both_halves(x):
        swapped = pltpu.roll(x, A_HEAD_DIM, axis=1)
        return jnp.where(low_head, x, swapped), jnp.where(low_head, swapped, x)

    def keys(kv_ref, cos, sin):
        k = both_halves(norm_rope(kv_ref[:, :LANES], kn_ref[...], cos, sin))
        v = both_halves(kv_ref[:, LANES:])
        return [(kg.astype(BF16), vg.astype(BF16)) for kg, vg in zip(k, v)]

    prev_kv = keys(kvp_ref, cosp_ref[...], sinp_ref[...])
    cur_kv = keys(kvc_ref, cos_c, sin_c)
    band = [(jnp.concatenate([kp, kc], axis=0), jnp.concatenate([vp, vc], axis=0))
            for (kp, vp), (kc, vc) in zip(prev_kv, cur_kv)]

    for p in range(A_PAIRS):
        kb, vb = band[(2 * p) // A_GROUP]
        q = norm_rope(q_ref[:, p * LANES:(p + 1) * LANES], qn_ref[...], cos_c, sin_c)
        outs = []
        for hh in range(2):
            qh = (jnp.where(low_head, q, 0.0) if hh == 0 else jnp.where(low_head, 0.0, q)).astype(BF16)
            s = jnp.where(valid, _dot_nt(qh, kb), NEG_BIG)
            sink = sink_ref[2 * p + hh]
            m = jnp.maximum(jnp.max(s, axis=-1, keepdims=True), sink)
            e = jnp.exp(s - m)
            denom = jnp.sum(e, axis=-1, keepdims=True) + jnp.exp(sink - m)
            outs.append(_dot(e.astype(BF16), vb) / denom)
        o = jnp.where(lane < A_HEAD_DIM, outs[0], outs[1])
        o_ref[:, p * LANES:(p + 1) * LANES] = (o * _silu(g_ref[:, p * LANES:(p + 1) * LANES])).astype(o_ref.dtype)


def _swa(qkv, gate, sinks, cos_t, sin_t, qn_l, kn_l, grp, batch, seq):
    nb = seq // A_BLOCK
    kv_w = 2 * A_KV_HEADS * A_HEAD_DIM
    kv_col = A_WIDTH // kv_w
    row = lambda b, i: b * nb + i
    prev = lambda i: jnp.maximum(i - 1, 0)
    return pl.pallas_call(
        _swa_body,
        grid=(batch, nb),
        in_specs=[
            pl.BlockSpec(memory_space=pltpu.SMEM),
            pl.BlockSpec((A_BLOCK, A_WIDTH), lambda b, i: (row(b, i), 0)),
            pl.BlockSpec((A_BLOCK, A_WIDTH), lambda b, i: (row(b, i), 0)),
            pl.BlockSpec((A_BLOCK, kv_w), lambda b, i: (row(b, prev(i)), kv_col)),
            pl.BlockSpec((A_BLOCK, kv_w), lambda b, i: (row(b, i), kv_col)),
            pl.BlockSpec((A_BLOCK, LANES), lambda b, i: (prev(i), 0)),
            pl.BlockSpec((A_BLOCK, LANES), lambda b, i: (prev(i), 0)),
            pl.BlockSpec((A_BLOCK, LANES), lambda b, i: (i, 0)),
            pl.BlockSpec((A_BLOCK, LANES), lambda b, i: (i, 0)),
            pl.BlockSpec((1, LANES), lambda b, i: (0, 0)),
            pl.BlockSpec((1, LANES), lambda b, i: (0, 0)),
            pl.BlockSpec((2 * LANES, LANES), lambda b, i: (0, 0)),
        ],
        out_specs=pl.BlockSpec((A_BLOCK, A_WIDTH), lambda b, i: (row(b, i), 0)),
        out_shape=jax.ShapeDtypeStruct((batch * seq, A_WIDTH), BF16),
        compiler_params=_cparams(("parallel", "parallel")),
        name="swa_mixer",
    )(sinks, qkv, gate, qkv, qkv, cos_t, sin_t, cos_t, sin_t, qn_l, kn_l, grp)


def _hgrn_body(q_ref, f_ref, v_ref, g_ref, lbp_ref, gn_ref, cum_ref, lvl_ref, o_ref, st_ref, *, layer):
    @pl.when(pl.program_id(1) == 0)
    def _():
        st_ref[...] = jnp.zeros_like(st_ref)

    lbp = lbp_ref[...]
    ex = jnp.exp(lbp - jnp.max(lbp, axis=0, keepdims=True))
    sm = ex / jnp.sum(ex, axis=0, keepdims=True)
    lb = jnp.sum(sm[:layer + 1], axis=0, keepdims=True) - sm[0:1]
    lbc = jnp.maximum(lb, LB_FLOOR)
    oml = 1.0 - lb

    s_pos, s_neg = _sigmoid_pair(f_ref[...])
    logf = jnp.log(lbc + oml * s_pos)
    kin = oml * s_neg - (lbc - lb)
    q = _silu(q_ref[...])

    sums = _dot(cum_ref[...], jnp.concatenate(_split_bf16(logf * LOG2E), axis=0))
    blk = q.shape[0]
    seg = lambda n: sums[n * blk:(n + 1) * blk]
    b_incl, b_rest = seg(B_LEVELS), seg(B_LEVELS + 1)
    b_last = b_incl[blk - 1:blk]

    def level_operand(lv):
        size = 1 << lv
        if size % SUBLANES == 0:
            x = jnp.concatenate([(q if n % 2 else kin)[n * size:(n + 1) * size] for n in range(blk // size)], axis=0)
        else:
            odd = ((lax.broadcasted_iota(jnp.int32, (1, SUBLANES, q.shape[1]), 1) >> lv) & 1) == 1
            grouped = (blk // SUBLANES, SUBLANES, q.shape[1])
            x = jnp.where(odd, q.reshape(grouped), kin.reshape(grouped)).reshape(q.shape)
        return (x * jnp.exp2(seg(lv))).astype(BF16)

    xs = [level_operand(lv) for lv in range(B_LEVELS)]
    q_bf, k_bf = q.astype(BF16), kin.astype(BF16)
    q_state = (q * jnp.exp2(b_incl)).astype(BF16)
    k_state = (kin * jnp.exp2(b_rest)).astype(BF16)
    decay = jnp.exp2(b_last)

    lvl = lvl_ref[...]
    at_level = [lvl == n for n in range(B_LEVELS + 1)]
    v = v_ref[...]
    for h in range(B_HEADS):
        sl = slice(h * B_HEAD_DIM, (h + 1) * B_HEAD_DIM)
        att = jnp.where(at_level[B_LEVELS], _dot_nt(q_bf[:, sl], k_bf[:, sl]), 0.0)
        for n in range(B_LEVELS):
            att = jnp.where(at_level[n], _dot_nt(xs[n][:, sl], xs[n][:, sl]), att)
        vh = v[:, sl]
        st = st_ref[h]
        o = _dot(att.astype(BF16), vh.astype(BF16)) + _dot_nt(q_state[:, sl], st.astype(BF16))
        st_ref[h] = st * decay[:, sl] + _dot(vh.T.astype(BF16), k_state[:, sl])
        ms = jnp.mean(o * o, axis=-1, keepdims=True)
        y = o * lax.rsqrt(ms + EPS) * gn_ref[...] * _silu(g_ref[:, sl])
        o_ref[:, sl] = y.astype(o_ref.dtype)


def _hgrn_constants():
    blk = B_BLOCK
    i = np.arange(blk)[:, None]
    j = np.arange(blk)[None, :]
    mats = []
    lvl = np.full((blk, blk), -1, np.int32)
    lvl[i == j] = B_LEVELS
    for lv in range(B_LEVELS):
        s = 1 << lv
        same = (i // s) == (j // s)
        odd = (i // s) % 2 == 1
        mats.append(same & np.where(odd, j <= i, j > i))
        lvl[odd & ((j // s) == (i // s) - 1)] = lv
    mats += [j <= i, j > i]
    cum = np.concatenate(mats, axis=0).astype(np.float32)
    return np.concatenate([cum, cum], axis=1), lvl


def _hgrn(proj, lower_bounds, gn, batch, seq, cols, layer):
    nb = seq // B_BLOCK
    cum, lvl = _hgrn_constants()
    row = lambda b, i: b * nb + i
    tile = lambda c: pl.BlockSpec((B_BLOCK, B_WIDTH), lambda b, i: (row(b, i), c))
    whole = lambda a: pl.BlockSpec(a.shape, lambda b, i: (0,) * a.ndim)
    gn2 = gn.reshape(1, B_HEAD_DIM)
    cum = jnp.asarray(cum, BF16)
    lvl = jnp.asarray(lvl)
    return pl.pallas_call(
        functools.partial(_hgrn_body, layer=layer),
        grid=(batch, nb),
        in_specs=[tile(cols[0]), tile(cols[1]), tile(cols[2]), tile(cols[3]),
                  whole(lower_bounds), whole(gn2), whole(cum), whole(lvl)],
        out_specs=pl.BlockSpec((B_BLOCK, B_WIDTH), lambda b, i: (row(b, i), 0)),
        out_shape=jax.ShapeDtypeStruct((batch * seq, B_WIDTH), BF16),
        scratch_shapes=[pltpu.VMEM((B_HEADS, B_HEAD_DIM, B_HEAD_DIM), F32)],
        compiler_params=_cparams(("parallel", "arbitrary")),
        name="hgrn2_mixer",
    )(proj, proj, proj, proj, lower_bounds, gn2, cum, lvl)


def _stick_tiles(qs, ks, vs, masks, tri2, carry_of):
    t = qs[0].shape[0]
    z2s, lhs = [], []
    for q, k_blk, mask in zip(qs, ks, masks):
        z2 = _dot_nt(q, k_blk)
        p = jnp.maximum(z2, 0.0) + jnp.log(1.0 + jnp.exp2(-jnp.abs(z2))) * LOG2E
        if mask is not None:
            p = jnp.where(mask, p, 0.0)
        z2s.append(z2)
        lhs.append(jnp.concatenate(_split_bf16(p), axis=1))
    incl = _dot(jnp.concatenate(lhs, axis=0), tri2)
    incls = [incl[n * t:(n + 1) * t] for n in range(len(qs))]
    totals = [x[:, 0:1] for x in incls]
    outs = []
    for n, (z2, v_blk, mask) in enumerate(zip(z2s, vs, masks)):
        w = jnp.exp2(z2 - incls[n] - carry_of(n, totals))
        if mask is not None:
            w = jnp.where(mask, w, 0.0)
        outs.append(_dot(w.astype(BF16), v_blk))
    return outs, totals


def _stick_body(q_ref, k_ref, v_ref, g_ref, tri_ref, o_ref):
    i = pl.program_id(2)
    t = q_ref.shape[0]
    heads = q_ref.shape[1] // C_HEAD_DIM
    hs = [slice(h * C_HEAD_DIM, (h + 1) * C_HEAD_DIM) for h in range(heads)]
    qs = [q_ref[:, s] for s in hs]
    tri2 = tri_ref[...]
    row = lax.broadcasted_iota(jnp.int32, (t, t), 0)
    col = lax.broadcasted_iota(jnp.int32, (t, t), 1)

    def key_tiles(kb):
        start = pl.multiple_of(kb * t, t)
        return ([k_ref[pl.ds(start, t), s] for s in hs], [v_ref[pl.ds(start, t), s] for s in hs])

    kd, vd = key_tiles(i)
    kp, vp = key_tiles(jnp.maximum(i - 1, 0))
    no_prev = jnp.where(i > 0, 0.0, C_NO_TILE)
    outs, totals = _stick_tiles(
        qs + qs, kd + kp, vd + vp, [col < row] * heads + [None] * heads, tri2,
        lambda n, tot: 0.0 if n < heads else tot[n - heads] + no_prev)
    accs = tuple(outs[h] + outs[heads + h] for h in range(heads))
    carries = tuple(totals[h] + totals[heads + h] for h in range(heads))

    def lowest(carries):
        return functools.reduce(jnp.minimum, [jnp.min(c) for c in carries])

    def more(state):
        n, _, _, min_carry = state
        return (n < i) & (min_carry < C_DONE_BITS)

    def step(state):
        n, accs, carries, _ = state
        kn, vn = key_tiles(i - 1 - n)
        outs, totals = _stick_tiles(qs, kn, vn, [None] * heads, tri2, lambda h, tot: carries[h])
        accs = tuple(a + o for a, o in zip(accs, outs))
        carries = tuple(c + tot for c, tot in zip(carries, totals))
        return n + 1, accs, carries, lowest(carries)

    _, accs, _, _ = lax.while_loop(more, step, (jnp.int32(1), accs, carries, lowest(carries)))
    for h in range(heads):
        o_ref[:, hs[h]] = (accs[h] * _silu(g_ref[:, hs[h]])).astype(o_ref.dtype)


def _stick(qkv, gate, batch, seq):
    t = C_TILE
    nq = seq // t
    hw = C_HEADS_PER_STEP * C_HEAD_DIM
    groups = C_HEADS // C_HEADS_PER_STEP
    tri = (np.arange(t)[:, None] >= np.arange(t)[None, :]).astype(np.float32)
    tri2 = jnp.asarray(np.concatenate([tri, tri], axis=0), BF16)
    row = lambda b, h, i: b * nq + i
    return pl.pallas_call(
        _stick_body,
        grid=(batch, groups, nq),
        in_specs=[
            pl.BlockSpec((t, hw), lambda b, h, i: (row(b, h, i), h)),
            pl.BlockSpec((seq, hw), lambda b, h, i: (b, groups + h)),
            pl.BlockSpec((seq, hw), lambda b, h, i: (b, 2 * groups + h)),
            pl.BlockSpec((t, hw), lambda b, h, i: (row(b, h, i), h)),
            pl.BlockSpec((2 * t, t), lambda b, h, i: (0, 0)),
        ],
        out_specs=pl.BlockSpec((t, hw), lambda b, h, i: (row(b, h, i), h)),
        out_shape=jax.ShapeDtypeStruct((batch * seq, C_WIDTH), BF16),
        compiler_params=_cparams(("parallel", "parallel", "arbitrary")),
        name="stick_breaking_mixer",
    )(qkv, qkv, qkv, gate, tri2)


def _swa_constants():
    lane = np.arange(LANES)
    grp = (lane[:, None] // A_HEAD_DIM == lane[None, :] // A_HEAD_DIM).astype(np.float32)
    return np.concatenate([grp, grp], axis=0)


def _rope_tables(seq):
    half = A_HALF
    inv_freq = 1.0 / (ROPE_THETA ** (jnp.arange(half, dtype=F32) / half))
    ang = jnp.arange(seq, dtype=F32)[:, None] * inv_freq[None, :]
    cos, sin = jnp.cos(ang), jnp.sin(ang)
    return jnp.tile(cos, (1, 4)), jnp.concatenate([-sin, sin, -sin, sin], axis=1)


def _even_layer(x2, h, batch, seq, w_in, qn, kn, sinks, lower_bounds, gn, w_out, layer, tables, next_ln):
    a_cols = A_WIDTH + 2 * A_KV_HEADS * A_HEAD_DIM
    rest = w_in.shape[2] - a_cols
    qkv, rest_proj = _inproj(h, w_in, layer, [(a_cols, F32), (rest, F32)], tm=1024, tn=a_cols)
    qn_l = (jnp.tile(qn, 2) * A_HEAD_DIM ** -0.5).reshape(1, LANES)
    kn_l = jnp.tile(kn, 2).reshape(1, LANES)
    ya = _swa(qkv, rest_proj, sinks, tables[0], tables[1], qn_l, kn_l,
              jnp.asarray(_swa_constants(), BF16), batch, seq)
    yb = _hgrn(rest_proj, lower_bounds, gn, batch, seq, (1, 2, 3, 4), layer)
    return _outproj(ya, yb, 0, 0, w_out, layer, x2, next_ln, tm=512)


def _odd_layer(x2, h, batch, seq, w_in, w_out, layer, next_ln):
    qkv, gate = _inproj(h, w_in, layer, [(3 * C_WIDTH, BF16), (C_WIDTH, F32)],
                        tm=512, tn=C_WIDTH, scaled_cols=C_WIDTH, scale=C_HEAD_DIM ** -0.5 * LOG2E)
    y = _stick(qkv, gate, batch, seq)
    return _outproj(y, y, 0, 1, w_out, layer, x2, next_ln, tm=512)


def kernel(x, ln_even, w_in_even, q_norm_a, k_norm_a, sinks_a, lower_bounds, g_norm_b, w_out_even,
           ln_odd, w_in_odd, w_out_odd):
    batch, seq, d = x.shape
    depth = ln_even.shape[0] + ln_odd.shape[0]
    x2 = x.reshape(batch * seq, d)
    tables = _rope_tables(seq)
    w_in_even, w_out_even = w_in_even.astype(BF16), w_out_even.astype(BF16)
    w_in_odd, w_out_odd = w_in_odd.astype(BF16), w_out_odd.astype(BF16)
    pre_norm = lambda layer: (ln_even if layer % 2 == 0 else ln_odd)[layer // 2]
    h = _norm(x2, pre_norm(0), tm=512)
    for layer in range(depth):
        e = layer // 2
        next_ln = pre_norm(layer + 1) if layer + 1 < depth else None
        if layer % 2 == 0:
            x2, h = _even_layer(x2, h, batch, seq, w_in_even, q_norm_a[e], k_norm_a[e], sinks_a[e],
                                lower_bounds, g_norm_b[e], w_out_even, e, tables, next_ln)
        else:
            x2, h = _odd_layer(x2, h, batch, seq, w_in_odd, w_out_odd, e, next_ln)
    return x2.reshape(batch, seq, d)
```

```python
import functools
import math

import jax
import jax.numpy as jnp
import numpy as np
from jax import lax
from jax.experimental import pallas as pl
from jax.experimental.pallas import tpu as pltpu

F32 = jnp.float32
BF16 = jnp.bfloat16

EPS = 1e-6
ROPE_THETA = 10000.0
LB_FLOOR = 1e-30
NEG_BIG = -1e30

V7X_VMEM_BYTES = 64 * 1024 * 1024
VMEM_LIMIT = V7X_VMEM_BYTES - 8 * 1024 * 1024
LANES = 128
SUBLANES = 8

A_HEAD_DIM = 64
A_HALF = A_HEAD_DIM // 2
A_HEADS = 16
A_KV_HEADS = 2
A_GROUP = A_HEADS // A_KV_HEADS
A_PAIRS = A_HEADS // 2
A_WIDTH = A_HEADS * A_HEAD_DIM
A_BLOCK = 128
B_HEAD_DIM = 128
B_HEADS = 8
B_WIDTH = B_HEADS * B_HEAD_DIM
B_BLOCK = 128
B_LEVELS = int(math.log2(B_BLOCK))
C_HEAD_DIM = 128
C_HEADS = 16
C_WIDTH = C_HEADS * C_HEAD_DIM
C_TILE = 256
C_HEADS_PER_STEP = 4
LOG2E = 1.0 / math.log(2.0)
C_DONE_BITS = 150.0
C_NO_TILE = 1e30


def _cparams(semantics):
    return pltpu.CompilerParams(dimension_semantics=semantics, vmem_limit_bytes=VMEM_LIMIT)


def _split_bf16(x):
    hi = x.astype(BF16)
    lo = (x - hi.astype(F32)).astype(BF16)
    return hi, lo


def _dot(a, b):
    return jnp.dot(a, b, preferred_element_type=F32)


def _dot_nt(a, b):
    return lax.dot_general(a, b, (((1,), (1,)), ((), ())), preferred_element_type=F32)


def _sigmoid_pair(x):
    half_t = 0.5 * jnp.tanh(0.5 * x)
    return 0.5 + half_t, 0.5 - half_t


def _silu(x):
    return x * _sigmoid_pair(x)[0]


def _rmsnorm_bf16(x, gain):
    ms = jnp.mean(x * x, axis=-1, keepdims=True)
    return (x * lax.rsqrt(ms + EPS) * gain).astype(BF16)


def _norm_body(x_ref, ln_ref, h_ref):
    h_ref[...] = _rmsnorm_bf16(x_ref[...], ln_ref[...])


def _norm(x2, ln, tm):
    t, d = x2.shape
    return pl.pallas_call(
        _norm_body,
        grid=(t // tm,),
        in_specs=[pl.BlockSpec((tm, d), lambda i: (i, 0)), pl.BlockSpec((1, d), lambda i: (0, 0))],
        out_specs=pl.BlockSpec((tm, d), lambda i: (i, 0)),
        out_shape=jax.ShapeDtypeStruct((t, d), BF16),
        compiler_params=_cparams(("parallel",)),
        name="rmsnorm",
    )(x2, ln.reshape(1, d))


def _inproj_body(h_ref, w_ref, *out_refs, seg_tiles, scaled_tiles, scale):
    j = pl.program_id(1)
    acc = _dot(h_ref[...], w_ref[...])
    if scaled_tiles:
        acc = acc * jnp.where(j < scaled_tiles, scale, 1.0)
    start = 0
    for o_ref, nt in zip(out_refs, seg_tiles):
        @pl.when((j >= start) & (j < start + nt))
        def _(o_ref=o_ref):
            o_ref[...] = acc.astype(o_ref.dtype)
        start += nt


def _inproj(h, w_bf16, layer, segs, tm, tn, scaled_cols=0, scale=1.0):
    t, d = h.shape
    n = w_bf16.shape[2]
    seg_tiles = tuple(wd // tn for wd, _ in segs)
    assert sum(seg_tiles) * tn == n and t % tm == 0 and scaled_cols % tn == 0
    out_specs, out_shapes, start = [], [], 0
    for (wd, dt), nt in zip(segs, seg_tiles):
        out_specs.append(pl.BlockSpec(
            (tm, tn), lambda i, j, s=start, m=nt - 1: (i, jnp.clip(j - s, 0, m))))
        out_shapes.append(jax.ShapeDtypeStruct((t, wd), dt))
        start += nt
    return pl.pallas_call(
        functools.partial(_inproj_body, seg_tiles=seg_tiles, scaled_tiles=scaled_cols // tn, scale=scale),
        grid=(t // tm, n // tn),
        in_specs=[
            pl.BlockSpec((tm, d), lambda i, j: (i, 0)),
            pl.BlockSpec((None, d, tn), lambda i, j: (layer, 0, j)),
        ],
        out_specs=out_specs,
        out_shape=out_shapes,
        compiler_params=_cparams(("parallel", "arbitrary")),
        name="inproj",
    )(h, w_bf16)


def _outproj_body(ya_ref, yb_ref, wa_ref, wb_ref, x_ref, *rest):
    x = x_ref[...] + _dot(ya_ref[...], wa_ref[...]) + _dot(yb_ref[...], wb_ref[...])
    if len(rest) == 1:
        rest[0][...] = x
    else:
        ln_ref, o_ref, h_ref = rest
        o_ref[...] = x
        h_ref[...] = _rmsnorm_bf16(x, ln_ref[...])


def _outproj(ya, yb, a_col, b_col, w_bf16, layer, x2, next_ln, tm):
    t, d = x2.shape
    half = w_bf16.shape[1] // 2
    row_tile = pl.BlockSpec((tm, d), lambda i: (i, 0))
    in_specs = [
        pl.BlockSpec((tm, half), lambda i: (i, a_col)),
        pl.BlockSpec((tm, half), lambda i: (i, b_col)),
        pl.BlockSpec((None, half, d), lambda i: (layer, 0, 0)),
        pl.BlockSpec((None, half, d), lambda i: (layer, 1, 0)),
        row_tile,
    ]
    args = [ya, yb, w_bf16, w_bf16, x2]
    out_specs, out_shape = row_tile, jax.ShapeDtypeStruct((t, d), F32)
    if next_ln is not None:
        in_specs.append(pl.BlockSpec((1, d), lambda i: (0, 0)))
        args.append(next_ln.reshape(1, d))
        out_specs, out_shape = [row_tile, row_tile], [out_shape, jax.ShapeDtypeStruct((t, d), BF16)]
    out = pl.pallas_call(
        _outproj_body,
        grid=(t // tm,),
        in_specs=in_specs,
        out_specs=out_specs,
        out_shape=out_shape,
        compiler_params=_cparams(("parallel",)),
        name="outproj_residual",
    )(*args)
    return (out, None) if next_ln is None else tuple(out)


def _swa_body(sink_ref, q_ref, g_ref, kvp_ref, kvc_ref, cosp_ref, sinp_ref, cosc_ref, sinc_ref,
              qn_ref, kn_ref, grp_ref, o_ref):
    i = pl.program_id(1)
    blk = q_ref.shape[0]
    lane = lax.broadcasted_iota(jnp.int32, (blk, LANES), 1)
    first_half = (lane // A_HALF) % 2 == 0
    low_head = lane < A_HEAD_DIM
    row = lax.broadcasted_iota(jnp.int32, (blk, 2 * blk), 0)
    col = lax.broadcasted_iota(jnp.int32, (blk, 2 * blk), 1)
    no_prev = jnp.where(i > 0, 0, 2 * blk)
    valid = ((col < blk) & (col > row + no_prev)) | ((col >= blk) & (col - blk <= row))
    cos_c, sin_c = cosc_ref[...], sinc_ref[...]
    grp = grp_ref[...]

    def norm_rope(x, gain, cos, sin):
        ms = _dot(jnp.concatenate(_split_bf16(x * x), axis=1), grp) * (1.0 / A_HEAD_DIM)
        x = x * lax.rsqrt(ms + EPS) * gain
        partner = jnp.where(first_half, pltpu.roll(x, LANES - A_HALF, axis=1), pltpu.roll(x, A_HALF, axis=1))
        return x * cos + partner * sin

    def both_halves(x):
        swapped = pltpu.roll(x, A_HEAD_DIM, axis=1)
        return jnp.where(low_head, x, swapped), jnp.where(low_head, swapped, x)

    def keys(kv_ref, cos, sin):
        k = both_halves(norm_rope(kv_ref[:, :LANES], kn_ref[...], cos, sin))
        v = both_halves(kv_ref[:, LANES:])
        return [(kg.astype(BF16), vg.astype(BF16)) for kg, vg in zip(k, v)]

    prev_kv = keys(kvp_ref, cosp_ref[...], sinp_ref[...])
    cur_kv = keys(kvc_ref, cos_c, sin_c)
    band = [(jnp.concatenate([kp, kc], axis=0), jnp.concatenate([vp, vc], axis=0))
            for (kp, vp), (kc, vc) in zip(prev_kv, cur_kv)]

    heads = range(A_HEADS)
    qs = [norm_rope(q_ref[:, p * LANES:(p + 1) * LANES], qn_ref[...], cos_c, sin_c) for p in range(A_PAIRS)]
    qh = [(jnp.where(low_head, qs[h // 2], 0.0) if h % 2 == 0 else jnp.where(low_head, 0.0, qs[h // 2])).astype(BF16)
          for h in heads]
    s = [jnp.where(valid, _dot_nt(qh[h], band[h // A_GROUP][0]), NEG_BIG) for h in heads]
    m = [jnp.maximum(jnp.max(s[h], axis=-1, keepdims=True), sink_ref[h]) for h in heads]
    e = [jnp.exp(s[h] - m[h]) for h in heads]
    denom = [jnp.sum(e[h], axis=-1, keepdims=True) + jnp.exp(sink_ref[h] - m[h]) for h in heads]
    outs = [_dot(e[h].astype(BF16), band[h // A_GROUP][1]) / denom[h] for h in heads]
    for p in range(A_PAIRS):
        o = jnp.where(low_head, outs[2 * p], outs[2 * p + 1])
        o_ref[:, p * LANES:(p + 1) * LANES] = (o * _silu(g_ref[:, p * LANES:(p + 1) * LANES])).astype(o_ref.dtype)


def _swa(qkv, gate, sinks, cos_t, sin_t, qn_l, kn_l, grp, batch, seq):
    nb = seq // A_BLOCK
    kv_w = 2 * A_KV_HEADS * A_HEAD_DIM
    kv_col = A_WIDTH // kv_w
    row = lambda b, i: b * nb + i
    prev = lambda i: jnp.maximum(i - 1, 0)
    return pl.pallas_call(
        _swa_body,
        grid=(batch, nb),
        in_specs=[
            pl.BlockSpec(memory_space=pltpu.SMEM),
            pl.BlockSpec((A_BLOCK, A_WIDTH), lambda b, i: (row(b, i), 0)),
            pl.BlockSpec((A_BLOCK, A_WIDTH), lambda b, i: (row(b, i), 0)),
            pl.BlockSpec((A_BLOCK, kv_w), lambda b, i: (row(b, prev(i)), kv_col)),
            pl.BlockSpec((A_BLOCK, kv_w), lambda b, i: (row(b, i), kv_col)),
            pl.BlockSpec((A_BLOCK, LANES), lambda b, i: (prev(i), 0)),
            pl.BlockSpec((A_BLOCK, LANES), lambda b, i: (prev(i), 0)),
            pl.BlockSpec((A_BLOCK, LANES), lambda b, i: (i, 0)),
            pl.BlockSpec((A_BLOCK, LANES), lambda b, i: (i, 0)),
            pl.BlockSpec((1, LANES), lambda b, i: (0, 0)),
            pl.BlockSpec((1, LANES), lambda b, i: (0, 0)),
            pl.BlockSpec((2 * LANES, LANES), lambda b, i: (0, 0)),
        ],
        out_specs=pl.BlockSpec((A_BLOCK, A_WIDTH), lambda b, i: (row(b, i), 0)),
        out_shape=jax.ShapeDtypeStruct((batch * seq, A_WIDTH), BF16),
        compiler_params=_cparams(("parallel", "parallel")),
        name="swa_mixer",
    )(sinks, qkv, gate, qkv, qkv, cos_t, sin_t, cos_t, sin_t, qn_l, kn_l, grp)


def _hgrn_body(q_ref, f_ref, v_ref, g_ref, lbp_ref, gn_ref, cum_ref, lvl_ref, o_ref, st_ref, *, layer):
    @pl.when(pl.program_id(1) == 0)
    def _():
        st_ref[...] = jnp.zeros_like(st_ref)

    lbp = lbp_ref[...]
    ex = jnp.exp(lbp - jnp.max(lbp, axis=0, keepdims=True))
    sm = ex / jnp.sum(ex, axis=0, keepdims=True)
    lb = jnp.sum(sm[:layer + 1], axis=0, keepdims=True) - sm[0:1]
    lbc = jnp.maximum(lb, LB_FLOOR)
    oml = 1.0 - lb

    s_pos, s_neg = _sigmoid_pair(f_ref[...])
    logf = jnp.log(lbc + oml * s_pos)
    kin = oml * s_neg - (lbc - lb)
    q = _silu(q_ref[...])

    sums = _dot(cum_ref[...], jnp.concatenate(_split_bf16(logf * LOG2E), axis=0))
    blk = q.shape[0]
    seg = lambda n: sums[n * blk:(n + 1) * blk]
    b_incl, b_rest = seg(B_LEVELS), seg(B_LEVELS + 1)
    b_last = b_incl[blk - 1:blk]

    def level_operand(lv):
        size = 1 << lv
        if size % SUBLANES == 0:
            x = jnp.concatenate([(q if n % 2 else kin)[n * size:(n + 1) * size] for n in range(blk // size)], axis=0)
        else:
            odd = ((lax.broadcasted_iota(jnp.int32, (1, SUBLANES, q.shape[1]), 1) >> lv) & 1) == 1
            grouped = (blk // SUBLANES, SUBLANES, q.shape[1])
            x = jnp.where(odd, q.reshape(grouped), kin.reshape(grouped)).reshape(q.shape)
        return (x * jnp.exp2(seg(lv))).astype(BF16)

    xs = [level_operand(lv) for lv in range(B_LEVELS)]
    q_bf, k_bf = q.astype(BF16), kin.astype(BF16)
    q_state = (q * jnp.exp2(b_incl)).astype(BF16)
    k_state = (kin * jnp.exp2(b_rest)).astype(BF16)
    decay = jnp.exp2(b_last)

    lvl = lvl_ref[...]
    at_level = [lvl == n for n in range(B_LEVELS + 1)]
    v = v_ref[...]
    heads = range(B_HEADS)
    hs = [slice(h * B_HEAD_DIM, (h + 1) * B_HEAD_DIM) for h in heads]
    att = [jnp.where(at_level[B_LEVELS], _dot_nt(q_bf[:, s], k_bf[:, s]), 0.0) for s in hs]
    for n in range(B_LEVELS):
        att = [jnp.where(at_level[n], _dot_nt(xs[n][:, s], xs[n][:, s]), a) for s, a in zip(hs, att)]
    st = [st_ref[h] for h in heads]
    o = [_dot(att[h].astype(BF16), v[:, hs[h]].astype(BF16)) + _dot_nt(q_state[:, hs[h]], st[h].astype(BF16))
         for h in heads]
    for h in heads:
        st_ref[h] = st[h] * decay[:, hs[h]] + _dot(v[:, hs[h]].T.astype(BF16), k_state[:, hs[h]])
    ms = [jnp.mean(o[h] * o[h], axis=-1, keepdims=True) for h in heads]
    for h in heads:
        y = o[h] * lax.rsqrt(ms[h] + EPS) * gn_ref[...] * _silu(g_ref[:, hs[h]])
        o_ref[:, hs[h]] = y.astype(o_ref.dtype)


def _hgrn_constants():
    blk = B_BLOCK
    i = np.arange(blk)[:, None]
    j = np.arange(blk)[None, :]
    mats = []
    lvl = np.full((blk, blk), -1, np.int32)
    lvl[i == j] = B_LEVELS
    for lv in range(B_LEVELS):
        s = 1 << lv
        same = (i // s) == (j // s)
        odd = (i // s) % 2 == 1
        mats.append(same & np.where(odd, j <= i, j > i))
        lvl[odd & ((j // s) == (i // s) - 1)] = lv
    mats += [j <= i, j > i]
    cum = np.concatenate(mats, axis=0).astype(np.float32)
    return np.concatenate([cum, cum], axis=1), lvl


def _hgrn(proj, lower_bounds, gn, batch, seq, cols, layer):
    nb = seq // B_BLOCK
    cum, lvl = _hgrn_constants()
    row = lambda b, i: b * nb + i
    tile = lambda c: pl.BlockSpec((B_BLOCK, B_WIDTH), lambda b, i: (row(b, i), c))
    whole = lambda a: pl.BlockSpec(a.shape, lambda b, i: (0,) * a.ndim)
    gn2 = gn.reshape(1, B_HEAD_DIM)
    cum = jnp.asarray(cum, BF16)
    lvl = jnp.asarray(lvl)
    return pl.pallas_call(
        functools.partial(_hgrn_body, layer=layer),
        grid=(batch, nb),
        in_specs=[tile(cols[0]), tile(cols[1]), tile(cols[2]), tile(cols[3]),
                  whole(lower_bounds), whole(gn2), whole(cum), whole(lvl)],
        out_specs=pl.BlockSpec((B_BLOCK, B_WIDTH), lambda b, i: (row(b, i), 0)),
        out_shape=jax.ShapeDtypeStruct((batch * seq, B_WIDTH), BF16),
        scratch_shapes=[pltpu.VMEM((B_HEADS, B_HEAD_DIM, B_HEAD_DIM), F32)],
        compiler_params=_cparams(("parallel", "arbitrary")),
        name="hgrn2_mixer",
    )(proj, proj, proj, proj, lower_bounds, gn2, cum, lvl)


def _stick_tiles(qs, ks, vs, masks, tri2, carry_of):
    t = qs[0].shape[0]
    keep = lambda x, mask: x if mask is None else jnp.where(mask, x, 0.0)
    z2s = [_dot_nt(q, k_blk) for q, k_blk in zip(qs, ks)]
    softplus = [jnp.maximum(z2, 0.0) + jnp.log(1.0 + jnp.exp2(-jnp.abs(z2))) * LOG2E for z2 in z2s]
    ps = [keep(p, mask) for p, mask in zip(softplus, masks)]
    incl = _dot(jnp.concatenate([jnp.concatenate(_split_bf16(p), axis=1) for p in ps], axis=0), tri2)
    incls = [incl[n * t:(n + 1) * t] for n in range(len(qs))]
    totals = [x[:, 0:1] for x in incls]
    ws = [keep(jnp.exp2(z2 - incls[n] - carry_of(n, totals)), mask).astype(BF16)
          for n, (z2, mask) in enumerate(zip(z2s, masks))]
    return [_dot(w, v_blk) for w, v_blk in zip(ws, vs)], totals


def _stick_body(q_ref, k_ref, v_ref, g_ref, tri_ref, o_ref):
    i = pl.program_id(2)
    t = q_ref.shape[0]
    heads = q_ref.shape[1] // C_HEAD_DIM
    hs = [slice(h * C_HEAD_DIM, (h + 1) * C_HEAD_DIM) for h in range(heads)]
    qs = [q_ref[:, s] for s in hs]
    tri2 = tri_ref[...]
    row = lax.broadcasted_iota(jnp.int32, (t, t), 0)
    col = lax.broadcasted_iota(jnp.int32, (t, t), 1)

    def key_tiles(kb):
        start = pl.multiple_of(kb * t, t)
        return ([k_ref[pl.ds(start, t), s] for s in hs], [v_ref[pl.ds(start, t), s] for s in hs])

    kd, vd = key_tiles(i)
    kp, vp = key_tiles(jnp.maximum(i - 1, 0))
    no_prev = jnp.where(i > 0, 0.0, C_NO_TILE)
    outs, totals = _stick_tiles(
        qs + qs, kd + kp, vd + vp, [col < row] * heads + [None] * heads, tri2,
        lambda n, tot: 0.0 if n < heads else tot[n - heads] + no_prev)
    accs = tuple(outs[h] + outs[heads + h] for h in range(heads))
    carries = tuple(totals[h] + totals[heads + h] for h in range(heads))

    def lowest(carries):
        return functools.reduce(jnp.minimum, [jnp.min(c) for c in carries])

    def more(state):
        n, _, _, min_carry = state
        return (n < i) & (min_carry < C_DONE_BITS)

    def step(state):
        n, accs, carries, _ = state
        kn, vn = key_tiles(i - 1 - n)
        outs, totals = _stick_tiles(qs, kn, vn, [None] * heads, tri2, lambda h, tot: carries[h])
        accs = tuple(a + o for a, o in zip(accs, outs))
        carries = tuple(c + tot for c, tot in zip(carries, totals))
        return n + 1, accs, carries, lowest(carries)

    _, accs, _, _ = lax.while_loop(more, step, (jnp.int32(1), accs, carries, lowest(carries)))
    for h in range(heads):
        o_ref[:, hs[h]] = (accs[h] * _silu(g_ref[:, hs[h]])).astype(o_ref.dtype)


def _stick(qkv, gate, batch, seq):
    t = C_TILE
    nq = seq // t
    hw = C_HEADS_PER_STEP * C_HEAD_DIM
    groups = C_HEADS // C_HEADS_PER_STEP
    tri = (np.arange(t)[:, None] >= np.arange(t)[None, :]).astype(np.float32)
    tri2 = jnp.asarray(np.concatenate([tri, tri], axis=0), BF16)
    row = lambda b, h, i: b * nq + i
    return pl.pallas_call(
        _stick_body,
        grid=(batch, groups, nq),
        in_specs=[
            pl.BlockSpec((t, hw), lambda b, h, i: (row(b, h, i), h)),
            pl.BlockSpec((seq, hw), lambda b, h, i: (b, groups + h)),
            pl.BlockSpec((seq, hw), lambda b, h, i: (b, 2 * groups + h)),
            pl.BlockSpec((t, hw), lambda b, h, i: (row(b, h, i), h)),
            pl.BlockSpec((2 * t, t), lambda b, h, i: (0, 0)),
        ],
        out_specs=pl.BlockSpec((t, hw), lambda b, h, i: (row(b, h, i), h)),
        out_shape=jax.ShapeDtypeStruct((batch * seq, C_WIDTH), BF16),
        compiler_params=_cparams(("parallel", "parallel", "arbitrary")),
        name="stick_breaking_mixer",
    )(qkv, qkv, qkv, gate, tri2)


def _swa_constants():
    lane = np.arange(LANES)
    grp = (lane[:, None] // A_HEAD_DIM == lane[None, :] // A_HEAD_DIM).astype(np.float32)
    return np.concatenate([grp, grp], axis=0)


def _rope_tables(seq):
    half = A_HALF
    inv_freq = 1.0 / (ROPE_THETA ** (jnp.arange(half, dtype=F32) / half))
    ang = jnp.arange(seq, dtype=F32)[:, None] * inv_freq[None, :]
    cos, sin = jnp.cos(ang), jnp.sin(ang)
    return jnp.tile(cos, (1, 4)), jnp.concatenate([-sin, sin, -sin, sin], axis=1)


def _even_layer(x2, h, batch, seq, w_in, qn, kn, sinks, lower_bounds, gn, w_out, layer, tables, next_ln):
    a_cols = A_WIDTH + 2 * A_KV_HEADS * A_HEAD_DIM
    rest = w_in.shape[2] - a_cols
    qkv, rest_proj = _inproj(h, w_in, layer, [(a_cols, F32), (rest, F32)], tm=1024, tn=a_cols)
    qn_l = (jnp.tile(qn, 2) * A_HEAD_DIM ** -0.5).reshape(1, LANES)
    kn_l = jnp.tile(kn, 2).reshape(1, LANES)
    ya = _swa(qkv, rest_proj, sinks, tables[0], tables[1], qn_l, kn_l,
              jnp.asarray(_swa_constants(), BF16), batch, seq)
    yb = _hgrn(rest_proj, lower_bounds, gn, batch, seq, (1, 2, 3, 4), layer)
    return _outproj(ya, yb, 0, 0, w_out, layer, x2, next_ln, tm=512)


def _odd_layer(x2, h, batch, seq, w_in, w_out, layer, next_ln):
    qkv, gate = _inproj(h, w_in, layer, [(3 * C_WIDTH, BF16), (C_WIDTH, F32)],
                        tm=512, tn=C_WIDTH, scaled_cols=C_WIDTH, scale=C_HEAD_DIM ** -0.5 * LOG2E)
    y = _stick(qkv, gate, batch, seq)
    return _outproj(y, y, 0, 1, w_out, layer, x2, next_ln, tm=512)


def kernel(x, ln_even, w_in_even, q_norm_a, k_norm_a, sinks_a, lower_bounds, g_norm_b, w_out_even,
           ln_odd, w_in_odd, w_out_odd):
    batch, seq, d = x.shape
    depth = ln_even.shape[0] + ln_odd.shape[0]
    x2 = x.reshape(batch * seq, d)
    tables = _rope_tables(seq)
    w_in_even, w_out_even = w_in_even.astype(BF16), w_out_even.astype(BF16)
    w_in_odd, w_out_odd = w_in_odd.astype(BF16), w_out_odd.astype(BF16)
    pre_norm = lambda layer: (ln_even if layer % 2 == 0 else ln_odd)[layer // 2]
    h = _norm(x2, pre_norm(0), tm=512)
    for layer in range(depth):
        e = layer // 2
        next_ln = pre_norm(layer + 1) if layer + 1 < depth else None
        if layer % 2 == 0:
            x2, h = _even_layer(x2, h, batch, seq, w_in_even, q_norm_a[e], k_norm_a[e], sinks_a[e],
                                lower_bounds, g_norm_b[e], w_out_even, e, tables, next_ln)
        else:
            x2, h = _odd_layer(x2, h, batch, seq, w_in_odd, w_out_odd, e, next_ln)
    return x2.reshape(batch, seq, d)
```

```python
import functools
import math

import jax
import jax.numpy as jnp
import numpy as np
from jax import lax
from jax.experimental import pallas as pl
from jax.experimental.pallas import tpu as pltpu

F32 = jnp.float32
BF16 = jnp.bfloat16

EPS = 1e-6
ROPE_THETA = 10000.0
LB_FLOOR = 1e-30
NEG_BIG = -1e30

V7X_VMEM_BYTES = 64 * 1024 * 1024
VMEM_LIMIT = V7X_VMEM_BYTES - 8 * 1024 * 1024
LANES = 128
SUBLANES = 8

A_HEAD_DIM = 64
A_HALF = A_HEAD_DIM // 2
A_HEADS = 16
A_KV_HEADS = 2
A_GROUP = A_HEADS // A_KV_HEADS
A_PAIRS = A_HEADS // 2
A_WIDTH = A_HEADS * A_HEAD_DIM
A_BLOCK = 128
B_HEAD_DIM = 128
B_HEADS = 8
B_WIDTH = B_HEADS * B_HEAD_DIM
B_BLOCK = 128
B_LEVELS = int(math.log2(B_BLOCK))
C_HEAD_DIM = 128
C_HEADS = 16
C_WIDTH = C_HEADS * C_HEAD_DIM
C_TILE = 256
C_HEADS_PER_STEP = 4
LOG2E = 1.0 / math.log(2.0)
C_DONE_BITS = 150.0
C_NO_TILE = 1e30


def _cparams(semantics):
    return pltpu.CompilerParams(dimension_semantics=semantics, vmem_limit_bytes=VMEM_LIMIT)


def _split_bf16(x):
    hi = x.astype(BF16)
    lo = (x - hi.astype(F32)).astype(BF16)
    return hi, lo


def _dot(a, b):
    return jnp.dot(a, b, preferred_element_type=F32)


def _dot_nt(a, b):
    return lax.dot_general(a, b, (((1,), (1,)), ((), ())), preferred_element_type=F32)


def _sigmoid_pair(x):
    half_t = 0.5 * jnp.tanh(0.5 * x)
    return 0.5 + half_t, 0.5 - half_t


def _silu(x):
    return x * _sigmoid_pair(x)[0]


def _rmsnorm_bf16(x, gain):
    ms = jnp.mean(x * x, axis=-1, keepdims=True)
    return (x * lax.rsqrt(ms + EPS) * gain).astype(BF16)


def _norm_body(x_ref, ln_ref, h_ref):
    h_ref[...] = _rmsnorm_bf16(x_ref[...], ln_ref[...])


def _norm(x2, ln, tm):
    t, d = x2.shape
    return pl.pallas_call(
        _norm_body,
        grid=(t // tm,),
        in_specs=[pl.BlockSpec((tm, d), lambda i: (i, 0)), pl.BlockSpec((1, d), lambda i: (0, 0))],
        out_specs=pl.BlockSpec((tm, d), lambda i: (i, 0)),
        out_shape=jax.ShapeDtypeStruct((t, d), BF16),
        compiler_params=_cparams(("parallel",)),
        name="rmsnorm",
    )(x2, ln.reshape(1, d))


def _inproj_body(h_ref, w_ref, *out_refs, seg_tiles, scaled_tiles, scale):
    j = pl.program_id(1)
    acc = _dot(h_ref[...], w_ref[...])
    if scaled_tiles:
        acc = acc * jnp.where(j < scaled_tiles, scale, 1.0)
    start = 0
    for o_ref, nt in zip(out_refs, seg_tiles):
        @pl.when((j >= start) & (j < start + nt))
        def _(o_ref=o_ref):
            o_ref[...] = acc.astype(o_ref.dtype)
        start += nt


def _inproj(h, w_bf16, layer, segs, tm, tn, scaled_cols=0, scale=1.0):
    t, d = h.shape
    n = w_bf16.shape[2]
    seg_tiles = tuple(wd // tn for wd, _ in segs)
    assert sum(seg_tiles) * tn == n and t % tm == 0 and scaled_cols % tn == 0
    out_specs, out_shapes, start = [], [], 0
    for (wd, dt), nt in zip(segs, seg_tiles):
        out_specs.append(pl.BlockSpec(
            (tm, tn), lambda i, j, s=start, m=nt - 1: (i, jnp.clip(j - s, 0, m))))
        out_shapes.append(jax.ShapeDtypeStruct((t, wd), dt))
        start += nt
    return pl.pallas_call(
        functools.partial(_inproj_body, seg_tiles=seg_tiles, scaled_tiles=scaled_cols // tn, scale=scale),
        grid=(t // tm, n // tn),
        in_specs=[
            pl.BlockSpec((tm, d), lambda i, j: (i, 0)),
            pl.BlockSpec((None, d, tn), lambda i, j: (layer, 0, j)),
        ],
        out_specs=out_specs,
        out_shape=out_shapes,
        compiler_params=_cparams(("parallel", "arbitrary")),
        name="inproj",
    )(h, w_bf16)


def _outproj_body(ya_ref, yb_ref, wa_ref, wb_ref, x_ref, *rest):
    x = x_ref[...] + _dot(ya_ref[...], wa_ref[...]) + _dot(yb_ref[...], wb_ref[...])
    if len(rest) == 1:
        rest[0][...] = x
    else:
        ln_ref, o_ref, h_ref = rest
        o_ref[...] = x
        h_ref[...] = _rmsnorm_bf16(x, ln_ref[...])


def _outproj(ya, yb, a_col, b_col, w_bf16, layer, x2, next_ln, tm):
    t, d = x2.shape
    half = w_bf16.shape[1] // 2
    row_tile = pl.BlockSpec((tm, d), lambda i: (i, 0))
    in_specs = [
        pl.BlockSpec((tm, half), lambda i: (i, a_col)),
        pl.BlockSpec((tm, half), lambda i: (i, b_col)),
        pl.BlockSpec((None, half, d), lambda i: (layer, 0, 0)),
        pl.BlockSpec((None, half, d), lambda i: (layer, 1, 0)),
        row_tile,
    ]
    args = [ya, yb, w_bf16, w_bf16, x2]
    out_specs, out_shape = row_tile, jax.ShapeDtypeStruct((t, d), F32)
    if next_ln is not None:
        in_specs.append(pl.BlockSpec((1, d), lambda i: (0, 0)))
        args.append(next_ln.reshape(1, d))
        out_specs, out_shape = [row_tile, row_tile], [out_shape, jax.ShapeDtypeStruct((t, d), BF16)]
    out = pl.pallas_call(
        _outproj_body,
        grid=(t // tm,),
        in_specs=in_specs,
        out_specs=out_specs,
        out_shape=out_shape,
        compiler_params=_cparams(("parallel",)),
        name="outproj_residual",
    )(*args)
    return (out, None) if next_ln is None else tuple(out)


def _swa_body(sink_ref, q_ref, g_ref, kvp_ref, kvc_ref, cosp_ref, sinp_ref, cosc_ref, sinc_ref,
              qn_ref, kn_ref, grp_ref, o_ref):
    i = pl.program_id(1)
    blk = q_ref.shape[0]
    lane = lax.broadcasted_iota(jnp.int32, (blk, LANES), 1)
    first_half = (lane // A_HALF) % 2 == 0
    low_head = lane < A_HEAD_DIM
    row = lax.broadcasted_iota(jnp.int32, (blk, 2 * blk), 0)
    col = lax.broadcasted_iota(jnp.int32, (blk, 2 * blk), 1)
    no_prev = jnp.where(i > 0, 0, 2 * blk)
    valid = ((col < blk) & (col > row + no_prev)) | ((col >= blk) & (col - blk <= row))
    cos_c, sin_c = cosc_ref[...], sinc_ref[...]
    grp = grp_ref[...]

    def norm_rope(x, gain, cos, sin):
        ms = _dot(jnp.concatenate(_split_bf16(x * x), axis=1), grp)
        x = x * lax.rsqrt(ms + EPS) * gain
        partner = jnp.where(first_half, pltpu.roll(x, LANES - A_HALF, axis=1), pltpu.roll(x, A_HALF, axis=1))
        return x * cos + partner * sin

    def both_halves(x):
        swapped = pltpu.roll(x, A_HEAD_DIM, axis=1)
        return jnp.where(low_head, x, swapped), jnp.where(low_head, swapped, x)

    def keys(kv_ref, cos, sin):
        k = both_halves(norm_rope(kv_ref[:, :LANES], kn_ref[...], cos, sin))
        v = both_halves(kv_ref[:, LANES:])
        return [(kg.astype(BF16), vg.astype(BF16)) for kg, vg in zip(k, v)]

    prev_kv = keys(kvp_ref, cosp_ref[...], sinp_ref[...])
    cur_kv = keys(kvc_ref, cos_c, sin_c)
    band = [(jnp.concatenate([kp, kc], axis=0), jnp.concatenate([vp, vc], axis=0))
            for (kp, vp), (kc, vc) in zip(prev_kv, cur_kv)]

    heads = range(A_HEADS)
    qs = [norm_rope(q_ref[:, p * LANES:(p + 1) * LANES], qn_ref[...], cos_c, sin_c) for p in range(A_PAIRS)]
    qh = [(jnp.where(low_head, qs[h // 2], 0.0) if h % 2 == 0 else jnp.where(low_head, 0.0, qs[h // 2])).astype(BF16)
          for h in heads]
    s = [jnp.where(valid, _dot_nt(qh[h], band[h // A_GROUP][0]), NEG_BIG) for h in heads]
    sink2 = [sink_ref[h] * LOG2E for h in heads]
    m = [jnp.maximum(jnp.max(s[h], axis=-1, keepdims=True), sink2[h]) for h in heads]
    e = [jnp.exp2(s[h] - m[h]) for h in heads]
    denom = [jnp.sum(e[h], axis=-1, keepdims=True) + jnp.exp2(sink2[h] - m[h]) for h in heads]
    outs = [_dot(e[h].astype(BF16), band[h // A_GROUP][1]) / denom[h] for h in heads]
    for p in range(A_PAIRS):
        o = jnp.where(low_head, outs[2 * p], outs[2 * p + 1])
        o_ref[:, p * LANES:(p + 1) * LANES] = (o * _silu(g_ref[:, p * LANES:(p + 1) * LANES])).astype(o_ref.dtype)


def _swa(qkv, gate, sinks, cos_t, sin_t, qn_l, kn_l, grp, batch, seq):
    nb = seq // A_BLOCK
    kv_w = 2 * A_KV_HEADS * A_HEAD_DIM
    kv_col = A_WIDTH // kv_w
    row = lambda b, i: b * nb + i
    prev = lambda i: jnp.maximum(i - 1, 0)
    return pl.pallas_call(
        _swa_body,
        grid=(batch, nb),
        in_specs=[
            pl.BlockSpec(memory_space=pltpu.SMEM),
            pl.BlockSpec((A_BLOCK, A_WIDTH), lambda b, i: (row(b, i), 0)),
            pl.BlockSpec((A_BLOCK, A_WIDTH), lambda b, i: (row(b, i), 0)),
            pl.BlockSpec((A_BLOCK, kv_w), lambda b, i: (row(b, prev(i)), kv_col)),
            pl.BlockSpec((A_BLOCK, kv_w), lambda b, i: (row(b, i), kv_col)),
            pl.BlockSpec((A_BLOCK, LANES), lambda b, i: (prev(i), 0)),
            pl.BlockSpec((A_BLOCK, LANES), lambda b, i: (prev(i), 0)),
            pl.BlockSpec((A_BLOCK, LANES), lambda b, i: (i, 0)),
            pl.BlockSpec((A_BLOCK, LANES), lambda b, i: (i, 0)),
            pl.BlockSpec((1, LANES), lambda b, i: (0, 0)),
            pl.BlockSpec((1, LANES), lambda b, i: (0, 0)),
            pl.BlockSpec((2 * LANES, LANES), lambda b, i: (0, 0)),
        ],
        out_specs=pl.BlockSpec((A_BLOCK, A_WIDTH), lambda b, i: (row(b, i), 0)),
        out_shape=jax.ShapeDtypeStruct((batch * seq, A_WIDTH), BF16),
        compiler_params=_cparams(("parallel", "parallel")),
        name="swa_mixer",
    )(sinks, qkv, gate, qkv, qkv, cos_t, sin_t, cos_t, sin_t, qn_l, kn_l, grp)


def _hgrn_body(q_ref, f_ref, v_ref, g_ref, lbp_ref, gn_ref, cum_ref, lvl_ref, o_ref, st_ref, *, layer):
    @pl.when(pl.program_id(1) == 0)
    def _():
        st_ref[...] = jnp.zeros_like(st_ref)

    lbp = lbp_ref[...]
    ex = jnp.exp(lbp - jnp.max(lbp, axis=0, keepdims=True))
    sm = ex / jnp.sum(ex, axis=0, keepdims=True)
    lb = jnp.sum(sm[:layer + 1], axis=0, keepdims=True) - sm[0:1]
    lbc = jnp.maximum(lb, LB_FLOOR)
    oml = 1.0 - lb

    s_pos, s_neg = _sigmoid_pair(f_ref[...])
    logf = jnp.log(lbc + oml * s_pos)
    kin = oml * s_neg - (lbc - lb)
    q = _silu(q_ref[...])

    sums = _dot(cum_ref[...], jnp.concatenate(_split_bf16(logf * LOG2E), axis=0))
    blk = q.shape[0]
    seg = lambda n: sums[n * blk:(n + 1) * blk]
    b_incl, b_rest = seg(B_LEVELS), seg(B_LEVELS + 1)
    b_last = b_incl[blk - 1:blk]

    def level_operand(lv):
        size = 1 << lv
        if size % SUBLANES == 0:
            x = jnp.concatenate([(q if n % 2 else kin)[n * size:(n + 1) * size] for n in range(blk // size)], axis=0)
        else:
            odd = ((lax.broadcasted_iota(jnp.int32, (1, SUBLANES, q.shape[1]), 1) >> lv) & 1) == 1
            grouped = (blk // SUBLANES, SUBLANES, q.shape[1])
            x = jnp.where(odd, q.reshape(grouped), kin.reshape(grouped)).reshape(q.shape)
        return (x * jnp.exp2(seg(lv))).astype(BF16)

    xs = [level_operand(lv) for lv in range(B_LEVELS)]
    q_bf, k_bf = q.astype(BF16), kin.astype(BF16)
    q_state = (q * jnp.exp2(b_incl)).astype(BF16)
    k_state = (kin * jnp.exp2(b_rest)).astype(BF16)
    decay = jnp.exp2(b_last)

    lvl = lvl_ref[...]
    at_level = [lvl == n for n in range(B_LEVELS + 1)]
    v = v_ref[...]
    heads = range(B_HEADS)
    hs = [slice(h * B_HEAD_DIM, (h + 1) * B_HEAD_DIM) for h in heads]
    att = [jnp.where(at_level[B_LEVELS], _dot_nt(q_bf[:, s], k_bf[:, s]), 0.0) for s in hs]
    for n in range(B_LEVELS):
        att = [jnp.where(at_level[n], _dot_nt(xs[n][:, s], xs[n][:, s]), a) for s, a in zip(hs, att)]
    st = [st_ref[h] for h in heads]
    o = [_dot(att[h].astype(BF16), v[:, hs[h]].astype(BF16)) + _dot_nt(q_state[:, hs[h]], st[h].astype(BF16))
         for h in heads]
    for h in heads:
        st_ref[h] = st[h] * decay[:, hs[h]] + _dot(v[:, hs[h]].T.astype(BF16), k_state[:, hs[h]])
    ms = [jnp.mean(o[h] * o[h], axis=-1, keepdims=True) for h in heads]
    for h in heads:
        y = o[h] * lax.rsqrt(ms[h] + EPS) * gn_ref[...] * _silu(g_ref[:, hs[h]])
        o_ref[:, hs[h]] = y.astype(o_ref.dtype)


def _hgrn_constants():
    blk = B_BLOCK
    i = np.arange(blk)[:, None]
    j = np.arange(blk)[None, :]
    mats = []
    lvl = np.full((blk, blk), -1, np.int32)
    lvl[i == j] = B_LEVELS
    for lv in range(B_LEVELS):
        s = 1 << lv
        same = (i // s) == (j // s)
        odd = (i // s) % 2 == 1
        mats.append(same & np.where(odd, j <= i, j > i))
        lvl[odd & ((j // s) == (i // s) - 1)] = lv
    mats += [j <= i, j > i]
    cum = np.concatenate(mats, axis=0).astype(np.float32)
    return np.concatenate([cum, cum], axis=1), lvl


def _hgrn(proj, lower_bounds, gn, batch, seq, cols, layer):
    nb = seq // B_BLOCK
    cum, lvl = _hgrn_constants()
    row = lambda b, i: b * nb + i
    tile = lambda c: pl.BlockSpec((B_BLOCK, B_WIDTH), lambda b, i: (row(b, i), c))
    whole = lambda a: pl.BlockSpec(a.shape, lambda b, i: (0,) * a.ndim)
    gn2 = gn.reshape(1, B_HEAD_DIM)
    cum = jnp.asarray(cum, BF16)
    lvl = jnp.asarray(lvl)
    return pl.pallas_call(
        functools.partial(_hgrn_body, layer=layer),
        grid=(batch, nb),
        in_specs=[tile(cols[0]), tile(cols[1]), tile(cols[2]), tile(cols[3]),
                  whole(lower_bounds), whole(gn2), whole(cum), whole(lvl)],
        out_specs=pl.BlockSpec((B_BLOCK, B_WIDTH), lambda b, i: (row(b, i), 0)),
        out_shape=jax.ShapeDtypeStruct((batch * seq, B_WIDTH), BF16),
        scratch_shapes=[pltpu.VMEM((B_HEADS, B_HEAD_DIM, B_HEAD_DIM), F32)],
        compiler_params=_cparams(("parallel", "arbitrary")),
        name="hgrn2_mixer",
    )(proj, proj, proj, proj, lower_bounds, gn2, cum, lvl)


def _stick_tiles(qs, ks, vs, masks, tri, carry_of):
    t = qs[0].shape[0]
    keep = lambda x, mask: x if mask is None else jnp.where(mask, x, 0.0)
    z2s = [_dot_nt(q, k_blk) for q, k_blk in zip(qs, ks)]
    softplus = [jnp.maximum(z2, 0.0) + jnp.log(1.0 + jnp.exp2(-jnp.abs(z2))) * LOG2E for z2 in z2s]
    ps = [keep(p, mask) for p, mask in zip(softplus, masks)]
    incl = _dot(jnp.concatenate([p.astype(BF16) for p in ps], axis=0), tri)
    incls = [incl[n * t:(n + 1) * t] for n in range(len(qs))]
    totals = [x[:, 0:1] for x in incls]
    ws = [keep(jnp.exp2(z2 - incls[n] - carry_of(n, totals)), mask).astype(BF16)
          for n, (z2, mask) in enumerate(zip(z2s, masks))]
    return [_dot(w, v_blk) for w, v_blk in zip(ws, vs)], totals


def _stick_body(q_ref, k_ref, v_ref, g_ref, tri_ref, o_ref):
    i = pl.program_id(2)
    t = q_ref.shape[0]
    heads = q_ref.shape[1] // C_HEAD_DIM
    hs = [slice(h * C_HEAD_DIM, (h + 1) * C_HEAD_DIM) for h in range(heads)]
    qs = [q_ref[:, s] for s in hs]
    tri2 = tri_ref[...]
    row = lax.broadcasted_iota(jnp.int32, (t, t), 0)
    col = lax.broadcasted_iota(jnp.int32, (t, t), 1)

    def key_tiles(kb):
        start = pl.multiple_of(kb * t, t)
        return ([k_ref[pl.ds(start, t), s] for s in hs], [v_ref[pl.ds(start, t), s] for s in hs])

    kd, vd = key_tiles(i)
    kp, vp = key_tiles(jnp.maximum(i - 1, 0))
    no_prev = jnp.where(i > 0, 0.0, C_NO_TILE)
    outs, totals = _stick_tiles(
        qs + qs, kd + kp, vd + vp, [col < row] * heads + [None] * heads, tri2,
        lambda n, tot: 0.0 if n < heads else tot[n - heads] + no_prev)
    accs = tuple(outs[h] + outs[heads + h] for h in range(heads))
    carries = tuple(totals[h] + totals[heads + h] for h in range(heads))

    def lowest(carries):
        return functools.reduce(jnp.minimum, [jnp.min(c) for c in carries])

    def more(state):
        n, _, _, min_carry = state
        return (n < i) & (min_carry < C_DONE_BITS)

    def step(state):
        n, accs, carries, _ = state
        kn, vn = key_tiles(i - 1 - n)
        outs, totals = _stick_tiles(qs, kn, vn, [None] * heads, tri2, lambda h, tot: carries[h])
        accs = tuple(a + o for a, o in zip(accs, outs))
        carries = tuple(c + tot for c, tot in zip(carries, totals))
        return n + 1, accs, carries, lowest(carries)

    _, accs, _, _ = lax.while_loop(more, step, (jnp.int32(1), accs, carries, lowest(carries)))
    for h in range(heads):
        o_ref[:, hs[h]] = (accs[h] * _silu(g_ref[:, hs[h]])).astype(o_ref.dtype)


def _stick(qkv, gate, batch, seq):
    t = C_TILE
    nq = seq // t
    hw = C_HEADS_PER_STEP * C_HEAD_DIM
    groups = C_HEADS // C_HEADS_PER_STEP
    tri = jnp.asarray(np.arange(t)[:, None] >= np.arange(t)[None, :], BF16)
    row = lambda b, h, i: b * nq + i
    return pl.pallas_call(
        _stick_body,
        grid=(batch, groups, nq),
        in_specs=[
            pl.BlockSpec((t, hw), lambda b, h, i: (row(b, h, i), h)),
            pl.BlockSpec((seq, hw), lambda b, h, i: (b, groups + h)),
            pl.BlockSpec((seq, hw), lambda b, h, i: (b, 2 * groups + h)),
            pl.BlockSpec((t, hw), lambda b, h, i: (row(b, h, i), h)),
            pl.BlockSpec((t, t), lambda b, h, i: (0, 0)),
        ],
        out_specs=pl.BlockSpec((t, hw), lambda b, h, i: (row(b, h, i), h)),
        out_shape=jax.ShapeDtypeStruct((batch * seq, C_WIDTH), BF16),
        compiler_params=_cparams(("parallel", "parallel", "arbitrary")),
        name="stick_breaking_mixer",
    )(qkv, qkv, qkv, gate, tri)


def _swa_constants():
    lane = np.arange(LANES)
    grp = (lane[:, None] // A_HEAD_DIM == lane[None, :] // A_HEAD_DIM).astype(np.float32) / A_HEAD_DIM
    return np.concatenate([grp, grp], axis=0)


def _rope_tables(seq):
    half = A_HALF
    inv_freq = 1.0 / (ROPE_THETA ** (jnp.arange(half, dtype=F32) / half))
    ang = jnp.arange(seq, dtype=F32)[:, None] * inv_freq[None, :]
    cos, sin = jnp.cos(ang), jnp.sin(ang)
    return jnp.tile(cos, (1, 4)), jnp.concatenate([-sin, sin, -sin, sin], axis=1)


def _even_layer(x2, h, batch, seq, w_in, qn, kn, sinks, lower_bounds, gn, w_out, layer, tables, next_ln):
    a_cols = A_WIDTH + 2 * A_KV_HEADS * A_HEAD_DIM
    rest = w_in.shape[2] - a_cols
    qkv, rest_proj = _inproj(h, w_in, layer, [(a_cols, F32), (rest, F32)], tm=1024, tn=a_cols)
    qn_l = (jnp.tile(qn, 2) * (A_HEAD_DIM ** -0.5 * LOG2E)).reshape(1, LANES)
    kn_l = jnp.tile(kn, 2).reshape(1, LANES)
    ya = _swa(qkv, rest_proj, sinks, tables[0], tables[1], qn_l, kn_l,
              jnp.asarray(_swa_constants(), BF16), batch, seq)
    yb = _hgrn(rest_proj, lower_bounds, gn, batch, seq, (1, 2, 3, 4), layer)
    return _outproj(ya, yb, 0, 0, w_out, layer, x2, next_ln, tm=512)


def _odd_layer(x2, h, batch, seq, w_in, w_out, layer, next_ln):
    qkv, gate = _inproj(h, w_in, layer, [(3 * C_WIDTH, BF16), (C_WIDTH, F32)],
                        tm=512, tn=C_WIDTH, scaled_cols=C_WIDTH, scale=C_HEAD_DIM ** -0.5 * LOG2E)
    y = _stick(qkv, gate, batch, seq)
    return _outproj(y, y, 0, 1, w_out, layer, x2, next_ln, tm=512)


def kernel(x, ln_even, w_in_even, q_norm_a, k_norm_a, sinks_a, lower_bounds, g_norm_b, w_out_even,
           ln_odd, w_in_odd, w_out_odd):
    batch, seq, d = x.shape
    depth = ln_even.shape[0] + ln_odd.shape[0]
    x2 = x.reshape(batch * seq, d)
    tables = _rope_tables(seq)
    w_in_even, w_out_even = w_in_even.astype(BF16), w_out_even.astype(BF16)
    w_in_odd, w_out_odd = w_in_odd.astype(BF16), w_out_odd.astype(BF16)
    pre_norm = lambda layer: (ln_even if layer % 2 == 0 else ln_odd)[layer // 2]
    h = _norm(x2, pre_norm(0), tm=512)
    for layer in range(depth):
        e = layer // 2
        next_ln = pre_norm(layer + 1) if layer + 1 < depth else None
        if layer % 2 == 0:
            x2, h = _even_layer(x2, h, batch, seq, w_in_even, q_norm_a[e], k_norm_a[e], sinks_a[e],
                                lower_bounds, g_norm_b[e], w_out_even, e, tables, next_ln)
        else:
            x2, h = _odd_layer(x2, h, batch, seq, w_in_odd, w_out_odd, e, next_ln)
    return x2.reshape(batch, seq, d)
```

```python
import functools
import math

import jax
import jax.numpy as jnp
import numpy as np
from jax import lax
from jax.experimental import pallas as pl
from jax.experimental.pallas import tpu as pltpu

F32 = jnp.float32
BF16 = jnp.bfloat16

EPS = 1e-6
ROPE_THETA = 10000.0
LB_FLOOR = 1e-30
NEG_BIG = -1e30

V7X_VMEM_BYTES = 64 * 1024 * 1024
VMEM_LIMIT = V7X_VMEM_BYTES - 8 * 1024 * 1024
LANES = 128
SUBLANES = 8

A_HEAD_DIM = 64
A_HALF = A_HEAD_DIM // 2
A_HEADS = 16
A_KV_HEADS = 2
A_GROUP = A_HEADS // A_KV_HEADS
A_PAIRS = A_HEADS // 2
A_WIDTH = A_HEADS * A_HEAD_DIM
A_BLOCK = 128
B_HEAD_DIM = 128
B_HEADS = 8
B_WIDTH = B_HEADS * B_HEAD_DIM
B_BLOCK = 128
B_LEVELS = int(math.log2(B_BLOCK))
C_HEAD_DIM = 128
C_HEADS = 16
C_WIDTH = C_HEADS * C_HEAD_DIM
C_TILE = 256
C_HEADS_PER_STEP = 4
LOG2E = 1.0 / math.log(2.0)
C_DONE_BITS = 150.0
C_NO_TILE = 1e30


def _cparams(semantics):
    return pltpu.CompilerParams(dimension_semantics=semantics, vmem_limit_bytes=VMEM_LIMIT)


def _split_bf16(x):
    hi = x.astype(BF16)
    lo = (x - hi.astype(F32)).astype(BF16)
    return hi, lo


def _dot(a, b):
    return jnp.dot(a, b, preferred_element_type=F32)


def _dot_nt(a, b):
    return lax.dot_general(a, b, (((1,), (1,)), ((), ())), preferred_element_type=F32)


def _sigmoid_pair(x):
    half_t = 0.5 * jnp.tanh(0.5 * x)
    return 0.5 + half_t, 0.5 - half_t


def _silu(x):
    return x * _sigmoid_pair(x)[0]


def _rmsnorm_bf16(x, gain):
    ms = jnp.mean(x * x, axis=-1, keepdims=True)
    return (x * lax.rsqrt(ms + EPS) * gain).astype(BF16)


def _norm_body(x_ref, ln_ref, h_ref):
    h_ref[...] = _rmsnorm_bf16(x_ref[...], ln_ref[...])


def _norm(x2, ln, tm):
    t, d = x2.shape
    return pl.pallas_call(
        _norm_body,
        grid=(t // tm,),
        in_specs=[pl.BlockSpec((tm, d), lambda i: (i, 0)), pl.BlockSpec((1, d), lambda i: (0, 0))],
        out_specs=pl.BlockSpec((tm, d), lambda i: (i, 0)),
        out_shape=jax.ShapeDtypeStruct((t, d), BF16),
        compiler_params=_cparams(("parallel",)),
        name="rmsnorm",
    )(x2, ln.reshape(1, d))


def _inproj_body(h_ref, w_ref, *out_refs, seg_tiles, scaled_tiles, scale):
    j = pl.program_id(1)
    acc = _dot(h_ref[...], w_ref[...])
    if scaled_tiles:
        acc = acc * jnp.where(j < scaled_tiles, scale, 1.0)
    start = 0
    for o_ref, nt in zip(out_refs, seg_tiles):
        @pl.when((j >= start) & (j < start + nt))
        def _(o_ref=o_ref):
            o_ref[...] = acc.astype(o_ref.dtype)
        start += nt


def _inproj(h, w_bf16, layer, segs, tm, tn, scaled_cols=0, scale=1.0):
    t, d = h.shape
    n = w_bf16.shape[2]
    seg_tiles = tuple(wd // tn for wd, _ in segs)
    assert sum(seg_tiles) * tn == n and t % tm == 0 and scaled_cols % tn == 0
    out_specs, out_shapes, start = [], [], 0
    for (wd, dt), nt in zip(segs, seg_tiles):
        out_specs.append(pl.BlockSpec(
            (tm, tn), lambda i, j, s=start, m=nt - 1: (i, jnp.clip(j - s, 0, m))))
        out_shapes.append(jax.ShapeDtypeStruct((t, wd), dt))
        start += nt
    return pl.pallas_call(
        functools.partial(_inproj_body, seg_tiles=seg_tiles, scaled_tiles=scaled_cols // tn, scale=scale),
        grid=(t // tm, n // tn),
        in_specs=[
            pl.BlockSpec((tm, d), lambda i, j: (i, 0)),
            pl.BlockSpec((None, d, tn), lambda i, j: (layer, 0, j)),
        ],
        out_specs=out_specs,
        out_shape=out_shapes,
        compiler_params=_cparams(("parallel", "arbitrary")),
        name="inproj",
    )(h, w_bf16)


def _outproj_body(ya_ref, yb_ref, wa_ref, wb_ref, x_ref, *rest):
    x = x_ref[...] + _dot(ya_ref[...], wa_ref[...]) + _dot(yb_ref[...], wb_ref[...])
    if len(rest) == 1:
        rest[0][...] = x
    else:
        ln_ref, o_ref, h_ref = rest
        o_ref[...] = x
        h_ref[...] = _rmsnorm_bf16(x, ln_ref[...])


def _outproj(ya, yb, a_col, b_col, w_bf16, layer, x2, next_ln, tm):
    t, d = x2.shape
    half = w_bf16.shape[1] // 2
    row_tile = pl.BlockSpec((tm, d), lambda i: (i, 0))
    in_specs = [
        pl.BlockSpec((tm, half), lambda i: (i, a_col)),
        pl.BlockSpec((tm, half), lambda i: (i, b_col)),
        pl.BlockSpec((None, half, d), lambda i: (layer, 0, 0)),
        pl.BlockSpec((None, half, d), lambda i: (layer, 1, 0)),
        row_tile,
    ]
    args = [ya, yb, w_bf16, w_bf16, x2]
    out_specs, out_shape = row_tile, jax.ShapeDtypeStruct((t, d), F32)
    if next_ln is not None:
        in_specs.append(pl.BlockSpec((1, d), lambda i: (0, 0)))
        args.append(next_ln.reshape(1, d))
        out_specs, out_shape = [row_tile, row_tile], [out_shape, jax.ShapeDtypeStruct((t, d), BF16)]
    out = pl.pallas_call(
        _outproj_body,
        grid=(t // tm,),
        in_specs=in_specs,
        out_specs=out_specs,
        out_shape=out_shape,
        compiler_params=_cparams(("parallel",)),
        name="outproj_residual",
    )(*args)
    return (out, None) if next_ln is None else tuple(out)


def _swa_body(sink_ref, q_ref, g_ref, kvp_ref, kvc_ref, cosp_ref, sinp_ref, cosc_ref, sinc_ref,
              qn_ref, kn_ref, grp_ref, o_ref):
    i = pl.program_id(1)
    blk = q_ref.shape[0]
    lane = lax.broadcasted_iota(jnp.int32, (blk, LANES), 1)
    first_half = (lane // A_HALF) % 2 == 0
    low_head = lane < A_HEAD_DIM
    row = lax.broadcasted_iota(jnp.int32, (blk, 2 * blk), 0)
    col = lax.broadcasted_iota(jnp.int32, (blk, 2 * blk), 1)
    no_prev = jnp.where(i > 0, 0, 2 * blk)
    valid = ((col < blk) & (col > row + no_prev)) | ((col >= blk) & (col - blk <= row))
    cos_c, sin_c = cosc_ref[...], sinc_ref[...]
    grp = grp_ref[...]

    def norm_rope(x, gain, cos, sin):
        ms = _dot(jnp.concatenate(_split_bf16(x * x), axis=1), grp)
        x = x * lax.rsqrt(ms + EPS) * gain
        partner = jnp.where(first_half, pltpu.roll(x, LANES - A_HALF, axis=1), pltpu.roll(x, A_HALF, axis=1))
        return x * cos + partner * sin

    def both_halves(x):
        swapped = pltpu.roll(x, A_HEAD_DIM, axis=1)
        return jnp.where(low_head, x, swapped), jnp.where(low_head, swapped, x)

    def keys(kv_ref, cos, sin):
        k = both_halves(norm_rope(kv_ref[:, :LANES], kn_ref[...], cos, sin))
        v = both_halves(kv_ref[:, LANES:])
        return [(kg.astype(BF16), vg.astype(BF16)) for kg, vg in zip(k, v)]

    prev_kv = keys(kvp_ref, cosp_ref[...], sinp_ref[...])
    cur_kv = keys(kvc_ref, cos_c, sin_c)
    band = [(jnp.concatenate([kp, kc], axis=0), jnp.concatenate([vp, vc], axis=0))
            for (kp, vp), (kc, vc) in zip(prev_kv, cur_kv)]

    heads = range(A_HEADS)
    qs = [norm_rope(q_ref[:, p * LANES:(p + 1) * LANES], qn_ref[...], cos_c, sin_c) for p in range(A_PAIRS)]
    qh = [(jnp.where(low_head, qs[h // 2], 0.0) if h % 2 == 0 else jnp.where(low_head, 0.0, qs[h // 2])).astype(BF16)
          for h in heads]
    s = [jnp.where(valid, _dot_nt(qh[h], band[h // A_GROUP][0]), NEG_BIG) for h in heads]
    sink2 = [sink_ref[h] * LOG2E for h in heads]
    m = [jnp.maximum(jnp.max(s[h], axis=-1, keepdims=True), sink2[h]) for h in heads]
    e = [jnp.exp2(s[h] - m[h]) for h in heads]
    denom = [jnp.sum(e[h], axis=-1, keepdims=True) + jnp.exp2(sink2[h] - m[h]) for h in heads]
    outs = [_dot(e[h].astype(BF16), band[h // A_GROUP][1]) / denom[h] for h in heads]
    for p in range(A_PAIRS):
        o = jnp.where(low_head, outs[2 * p], outs[2 * p + 1])
        o_ref[:, p * LANES:(p + 1) * LANES] = (o * _silu(g_ref[:, p * LANES:(p + 1) * LANES])).astype(o_ref.dtype)


def _hgrn_body(q_ref, f_ref, v_ref, g_ref, lbp_ref, gn_ref, cum_ref, lvl_ref, o_ref, st_ref, *, layer,
               before_matmuls=None):
    @pl.when(pl.program_id(1) == 0)
    def _():
        st_ref[...] = jnp.zeros_like(st_ref)

    lbp = lbp_ref[...]
    ex = jnp.exp(lbp - jnp.max(lbp, axis=0, keepdims=True))
    sm = ex / jnp.sum(ex, axis=0, keepdims=True)
    lb = jnp.sum(sm[:layer + 1], axis=0, keepdims=True) - sm[0:1]
    lbc = jnp.maximum(lb, LB_FLOOR)
    oml = 1.0 - lb

    s_pos, s_neg = _sigmoid_pair(f_ref[...])
    logf = jnp.log(lbc + oml * s_pos)
    kin = oml * s_neg - (lbc - lb)
    q = _silu(q_ref[...])

    sums = _dot(cum_ref[...], jnp.concatenate(_split_bf16(logf * LOG2E), axis=0))
    blk = q.shape[0]
    seg = lambda n: sums[n * blk:(n + 1) * blk]
    b_incl, b_rest = seg(B_LEVELS), seg(B_LEVELS + 1)
    b_last = b_incl[blk - 1:blk]

    def level_operand(lv):
        size = 1 << lv
        if size % SUBLANES == 0:
            x = jnp.concatenate([(q if n % 2 else kin)[n * size:(n + 1) * size] for n in range(blk // size)], axis=0)
        else:
            odd = ((lax.broadcasted_iota(jnp.int32, (1, SUBLANES, q.shape[1]), 1) >> lv) & 1) == 1
            grouped = (blk // SUBLANES, SUBLANES, q.shape[1])
            x = jnp.where(odd, q.reshape(grouped), kin.reshape(grouped)).reshape(q.shape)
        return (x * jnp.exp2(seg(lv))).astype(BF16)

    xs = [level_operand(lv) for lv in range(B_LEVELS)]
    q_bf, k_bf = q.astype(BF16), kin.astype(BF16)
    q_state = (q * jnp.exp2(b_incl)).astype(BF16)
    k_state = (kin * jnp.exp2(b_rest)).astype(BF16)
    decay = jnp.exp2(b_last)
    if before_matmuls is not None:
        before_matmuls()

    lvl = lvl_ref[...]
    at_level = [lvl == n for n in range(B_LEVELS + 1)]
    v = v_ref[...]
    heads = range(B_HEADS)
    hs = [slice(h * B_HEAD_DIM, (h + 1) * B_HEAD_DIM) for h in heads]
    att = [jnp.where(at_level[B_LEVELS], _dot_nt(q_bf[:, s], k_bf[:, s]), 0.0) for s in hs]
    for n in range(B_LEVELS):
        att = [jnp.where(at_level[n], _dot_nt(xs[n][:, s], xs[n][:, s]), a) for s, a in zip(hs, att)]
    st = [st_ref[h] for h in heads]
    o = [_dot(att[h].astype(BF16), v[:, hs[h]].astype(BF16)) + _dot_nt(q_state[:, hs[h]], st[h].astype(BF16))
         for h in heads]
    for h in heads:
        st_ref[h] = st[h] * decay[:, hs[h]] + _dot(v[:, hs[h]].T.astype(BF16), k_state[:, hs[h]])
    ms = [jnp.mean(o[h] * o[h], axis=-1, keepdims=True) for h in heads]
    for h in heads:
        y = o[h] * lax.rsqrt(ms[h] + EPS) * gn_ref[...] * _silu(g_ref[:, hs[h]])
        o_ref[:, hs[h]] = y.astype(o_ref.dtype)


def _hgrn_constants():
    blk = B_BLOCK
    i = np.arange(blk)[:, None]
    j = np.arange(blk)[None, :]
    mats = []
    lvl = np.full((blk, blk), -1, np.int32)
    lvl[i == j] = B_LEVELS
    for lv in range(B_LEVELS):
        s = 1 << lv
        same = (i // s) == (j // s)
        odd = (i // s) % 2 == 1
        mats.append(same & np.where(odd, j <= i, j > i))
        lvl[odd & ((j // s) == (i // s) - 1)] = lv
    mats += [j <= i, j > i]
    cum = np.concatenate(mats, axis=0).astype(np.float32)
    return np.concatenate([cum, cum], axis=1), lvl


N_SWA_IN = 12


def _even_mixers_body(*refs, layer):
    swa_in, rest = refs[:N_SWA_IN], refs[N_SWA_IN:]
    hgrn_in, (ya_ref, yb_ref, st_ref) = rest[:8], rest[8:]
    _hgrn_body(*hgrn_in, yb_ref, st_ref, layer=layer, before_matmuls=lambda: _swa_body(*swa_in, ya_ref))


def _even_mixers(qkv, rest_proj, sinks, cos_t, sin_t, qn_l, kn_l, grp, lower_bounds, gn, batch, seq, layer):
    assert A_BLOCK == B_BLOCK
    nb = seq // A_BLOCK
    kv_w = 2 * A_KV_HEADS * A_HEAD_DIM
    kv_col = A_WIDTH // kv_w
    row = lambda b, i: b * nb + i
    prev = lambda i: jnp.maximum(i - 1, 0)
    cum, lvl = _hgrn_constants()
    cum, lvl, gn2 = jnp.asarray(cum, BF16), jnp.asarray(lvl), gn.reshape(1, B_HEAD_DIM)
    wide = lambda c: pl.BlockSpec((A_BLOCK, A_WIDTH), lambda b, i: (row(b, i), c))
    table = lambda at: pl.BlockSpec((A_BLOCK, LANES), lambda b, i: (at(i), 0))
    whole = lambda a: pl.BlockSpec(a.shape, lambda b, i: (0,) * a.ndim)
    swa_specs = [
        pl.BlockSpec(memory_space=pltpu.SMEM), wide(0), wide(0),
        pl.BlockSpec((A_BLOCK, kv_w), lambda b, i: (row(b, prev(i)), kv_col)),
        pl.BlockSpec((A_BLOCK, kv_w), lambda b, i: (row(b, i), kv_col)),
        table(prev), table(prev), table(lambda i: i), table(lambda i: i),
        whole(qn_l), whole(kn_l), whole(grp),
    ]
    assert len(swa_specs) == N_SWA_IN
    hgrn_specs = [wide(1), wide(2), wide(3), wide(4), whole(lower_bounds), whole(gn2), whole(cum), whole(lvl)]
    out = jax.ShapeDtypeStruct((batch * seq, A_WIDTH), BF16)
    return pl.pallas_call(
        functools.partial(_even_mixers_body, layer=layer),
        grid=(batch, nb),
        in_specs=swa_specs + hgrn_specs,
        out_specs=[wide(0), wide(0)],
        out_shape=[out, out],
        scratch_shapes=[pltpu.VMEM((B_HEADS, B_HEAD_DIM, B_HEAD_DIM), F32)],
        compiler_params=_cparams(("parallel", "arbitrary")),
        name="even_mixers",
    )(sinks, qkv, rest_proj, qkv, qkv, cos_t, sin_t, cos_t, sin_t, qn_l, kn_l, grp,
      rest_proj, rest_proj, rest_proj, rest_proj, lower_bounds, gn2, cum, lvl)


def _stick_tiles(qs, ks, vs, masks, tri, carry_of):
    t = qs[0].shape[0]
    keep = lambda x, mask: x if mask is None else jnp.where(mask, x, 0.0)
    z2s = [_dot_nt(q, k_blk) for q, k_blk in zip(qs, ks)]
    softplus = [jnp.maximum(z2, 0.0) + jnp.log(1.0 + jnp.exp2(-jnp.abs(z2))) * LOG2E for z2 in z2s]
    ps = [keep(p, mask) for p, mask in zip(softplus, masks)]
    incl = _dot(jnp.concatenate([p.astype(BF16) for p in ps], axis=0), tri)
    incls = [incl[n * t:(n + 1) * t] for n in range(len(qs))]
    totals = [x[:, 0:1] for x in incls]
    ws = [keep(jnp.exp2(z2 - incls[n] - carry_of(n, totals)), mask).astype(BF16)
          for n, (z2, mask) in enumerate(zip(z2s, masks))]
    return [_dot(w, v_blk) for w, v_blk in zip(ws, vs)], totals


def _stick_body(q_ref, k_ref, v_ref, g_ref, tri_ref, o_ref):
    i = pl.program_id(2)
    t = q_ref.shape[0]
    heads = q_ref.shape[1] // C_HEAD_DIM
    hs = [slice(h * C_HEAD_DIM, (h + 1) * C_HEAD_DIM) for h in range(heads)]
    qs = [q_ref[:, s] for s in hs]
    tri2 = tri_ref[...]
    row = lax.broadcasted_iota(jnp.int32, (t, t), 0)
    col = lax.broadcasted_iota(jnp.int32, (t, t), 1)

    def key_tiles(kb):
        start = pl.multiple_of(kb * t, t)
        return ([k_ref[pl.ds(start, t), s] for s in hs], [v_ref[pl.ds(start, t), s] for s in hs])

    kd, vd = key_tiles(i)
    kp, vp = key_tiles(jnp.maximum(i - 1, 0))
    no_prev = jnp.where(i > 0, 0.0, C_NO_TILE)
    outs, totals = _stick_tiles(
        qs + qs, kd + kp, vd + vp, [col < row] * heads + [None] * heads, tri2,
        lambda n, tot: 0.0 if n < heads else tot[n - heads] + no_prev)
    accs = tuple(outs[h] + outs[heads + h] for h in range(heads))
    carries = tuple(totals[h] + totals[heads + h] for h in range(heads))

    def lowest(carries):
        return functools.reduce(jnp.minimum, [jnp.min(c) for c in carries])

    def more(state):
        n, _, _, min_carry = state
        return (n < i) & (min_carry < C_DONE_BITS)

    def step(state):
        n, accs, carries, _ = state
        kn, vn = key_tiles(i - 1 - n)
        outs, totals = _stick_tiles(qs, kn, vn, [None] * heads, tri2, lambda h, tot: carries[h])
        accs = tuple(a + o for a, o in zip(accs, outs))
        carries = tuple(c + tot for c, tot in zip(carries, totals))
        return n + 1, accs, carries, lowest(carries)

    _, accs, _, _ = lax.while_loop(more, step, (jnp.int32(1), accs, carries, lowest(carries)))
    for h in range(heads):
        o_ref[:, hs[h]] = (accs[h] * _silu(g_ref[:, hs[h]])).astype(o_ref.dtype)


def _stick(qkv, gate, batch, seq):
    t = C_TILE
    nq = seq // t
    hw = C_HEADS_PER_STEP * C_HEAD_DIM
    groups = C_HEADS // C_HEADS_PER_STEP
    tri = jnp.asarray(np.arange(t)[:, None] >= np.arange(t)[None, :], BF16)
    row = lambda b, h, i: b * nq + i
    return pl.pallas_call(
        _stick_body,
        grid=(batch, groups, nq),
        in_specs=[
            pl.BlockSpec((t, hw), lambda b, h, i: (row(b, h, i), h)),
            pl.BlockSpec((seq, hw), lambda b, h, i: (b, groups + h)),
            pl.BlockSpec((seq, hw), lambda b, h, i: (b, 2 * groups + h)),
            pl.BlockSpec((t, hw), lambda b, h, i: (row(b, h, i), h)),
            pl.BlockSpec((t, t), lambda b, h, i: (0, 0)),
        ],
        out_specs=pl.BlockSpec((t, hw), lambda b, h, i: (row(b, h, i), h)),
        out_shape=jax.ShapeDtypeStruct((batch * seq, C_WIDTH), BF16),
        compiler_params=_cparams(("parallel", "parallel", "arbitrary")),
        name="stick_breaking_mixer",
    )(qkv, qkv, qkv, gate, tri)


def _swa_constants():
    lane = np.arange(LANES)
    grp = (lane[:, None] // A_HEAD_DIM == lane[None, :] // A_HEAD_DIM).astype(np.float32) / A_HEAD_DIM
    return np.concatenate([grp, grp], axis=0)


def _rope_tables(seq):
    half = A_HALF
    inv_freq = 1.0 / (ROPE_THETA ** (jnp.arange(half, dtype=F32) / half))
    ang = jnp.arange(seq, dtype=F32)[:, None] * inv_freq[None, :]
    cos, sin = jnp.cos(ang), jnp.sin(ang)
    return jnp.tile(cos, (1, 4)), jnp.concatenate([-sin, sin, -sin, sin], axis=1)


def _even_layer(x2, h, batch, seq, w_in, qn, kn, sinks, lower_bounds, gn, w_out, layer, tables, next_ln):
    a_cols = A_WIDTH + 2 * A_KV_HEADS * A_HEAD_DIM
    rest = w_in.shape[2] - a_cols
    qkv, rest_proj = _inproj(h, w_in, layer, [(a_cols, F32), (rest, F32)], tm=1024, tn=a_cols)
    qn_l = (jnp.tile(qn, 2) * (A_HEAD_DIM ** -0.5 * LOG2E)).reshape(1, LANES)
    kn_l = jnp.tile(kn, 2).reshape(1, LANES)
    ya, yb = _even_mixers(qkv, rest_proj, sinks, tables[0], tables[1], qn_l, kn_l,
                          jnp.asarray(_swa_constants(), BF16), lower_bounds, gn, batch, seq, layer)
    return _outproj(ya, yb, 0, 0, w_out, layer, x2, next_ln, tm=512)


def _odd_layer(x2, h, batch, seq, w_in, w_out, layer, next_ln):
    qkv, gate = _inproj(h, w_in, layer, [(3 * C_WIDTH, BF16), (C_WIDTH, F32)],
                        tm=512, tn=C_WIDTH, scaled_cols=C_WIDTH, scale=C_HEAD_DIM ** -0.5 * LOG2E)
    y = _stick(qkv, gate, batch, seq)
    return _outproj(y, y, 0, 1, w_out, layer, x2, next_ln, tm=512)


def kernel(x, ln_even, w_in_even, q_norm_a, k_norm_a, sinks_a, lower_bounds, g_norm_b, w_out_even,
           ln_odd, w_in_odd, w_out_odd):
    batch, seq, d = x.shape
    depth = ln_even.shape[0] + ln_odd.shape[0]
    x2 = x.reshape(batch * seq, d)
    tables = _rope_tables(seq)
    w_in_even, w_out_even = w_in_even.astype(BF16), w_out_even.astype(BF16)
    w_in_odd, w_out_odd = w_in_odd.astype(BF16), w_out_odd.astype(BF16)
    pre_norm = lambda layer: (ln_even if layer % 2 == 0 else ln_odd)[layer // 2]
    h = _norm(x2, pre_norm(0), tm=512)
    for layer in range(depth):
        e = layer // 2
        next_ln = pre_norm(layer + 1) if layer + 1 < depth else None
        if layer % 2 == 0:
            x2, h = _even_layer(x2, h, batch, seq, w_in_even, q_norm_a[e], k_norm_a[e], sinks_a[e],
                                lower_bounds, g_norm_b[e], w_out_even, e, tables, next_ln)
        else:
            x2, h = _odd_layer(x2, h, batch, seq, w_in_odd, w_out_odd, e, next_ln)
    return x2.reshape(batch, seq, d)
```

```python
import functools
import math

import jax
import jax.numpy as jnp
import numpy as np
from jax import lax
from jax.experimental import pallas as pl
from jax.experimental.pallas import tpu as pltpu

F32 = jnp.float32
BF16 = jnp.bfloat16

EPS = 1e-6
ROPE_THETA = 10000.0
LB_FLOOR = 1e-30
NEG_BIG = -1e30

V7X_VMEM_BYTES = 64 * 1024 * 1024
VMEM_LIMIT = V7X_VMEM_BYTES - 8 * 1024 * 1024
LANES = 128
SUBLANES = 8
INPROJ_CHUNK = 512

A_HEAD_DIM = 64
A_HALF = A_HEAD_DIM // 2
A_HEADS = 16
A_KV_HEADS = 2
A_GROUP = A_HEADS // A_KV_HEADS
A_PAIRS = A_HEADS // 2
A_WIDTH = A_HEADS * A_HEAD_DIM
A_BLOCK = 128
B_HEAD_DIM = 128
B_HEADS = 8
B_WIDTH = B_HEADS * B_HEAD_DIM
B_BLOCK = 128
B_LEVELS = int(math.log2(B_BLOCK))
C_HEAD_DIM = 128
C_HEADS = 16
C_WIDTH = C_HEADS * C_HEAD_DIM
C_TILE = 256
C_HEADS_PER_STEP = 4
LOG2E = 1.0 / math.log(2.0)
C_DONE_BITS = 150.0
C_NO_TILE = 1e30


def _cparams(semantics):
    return pltpu.CompilerParams(dimension_semantics=semantics, vmem_limit_bytes=VMEM_LIMIT)


def _split_bf16(x):
    hi = x.astype(BF16)
    lo = (x - hi.astype(F32)).astype(BF16)
    return hi, lo


def _dot(a, b):
    return jnp.dot(a, b, preferred_element_type=F32)


def _dot_nt(a, b):
    return lax.dot_general(a, b, (((1,), (1,)), ((), ())), preferred_element_type=F32)


def _sigmoid_pair(x):
    half_t = 0.5 * jnp.tanh(0.5 * x)
    return 0.5 + half_t, 0.5 - half_t


def _silu(x):
    return x * _sigmoid_pair(x)[0]


def _rmsnorm_bf16(x, gain):
    ms = jnp.mean(x * x, axis=-1, keepdims=True)
    return (x * lax.rsqrt(ms + EPS) * gain).astype(BF16)


def _norm_body(x_ref, ln_ref, h_ref):
    h_ref[...] = _rmsnorm_bf16(x_ref[...], ln_ref[...])


def _norm(x2, ln, tm):
    t, d = x2.shape
    return pl.pallas_call(
        _norm_body,
        grid=(t // tm,),
        in_specs=[pl.BlockSpec((tm, d), lambda i: (i, 0)), pl.BlockSpec((1, d), lambda i: (0, 0))],
        out_specs=pl.BlockSpec((tm, d), lambda i: (i, 0)),
        out_shape=jax.ShapeDtypeStruct((t, d), BF16),
        compiler_params=_cparams(("parallel",)),
        name="rmsnorm",
    )(x2, ln.reshape(1, d))


def _inproj_body(h_ref, w_ref, *out_refs, seg_tiles, scaled_tiles, scale):
    j = pl.program_id(1)
    tn = w_ref.shape[1]
    chunk = math.gcd(tn, INPROJ_CHUNK)
    start = 0
    for o_ref, nt in zip(out_refs, seg_tiles):
        @pl.when((j >= start) & (j < start + nt))
        def _(o_ref=o_ref):
            for c in range(0, tn, chunk):
                acc = _dot(h_ref[...], w_ref[:, c:c + chunk])
                if scaled_tiles:
                    acc = acc * jnp.where(j < scaled_tiles, scale, 1.0)
                o_ref[:, c:c + chunk] = acc.astype(o_ref.dtype)
        start += nt


def _inproj(h, w_bf16, layer, segs, tm, tn, scaled_cols=0, scale=1.0):
    t, d = h.shape
    n = w_bf16.shape[2]
    seg_tiles = tuple(wd // tn for wd, _ in segs)
    assert sum(seg_tiles) * tn == n and t % tm == 0 and scaled_cols % tn == 0
    out_specs, out_shapes, start = [], [], 0
    for (wd, dt), nt in zip(segs, seg_tiles):
        out_specs.append(pl.BlockSpec(
            (tm, tn), lambda i, j, s=start, m=nt - 1: (i, jnp.clip(j - s, 0, m))))
        out_shapes.append(jax.ShapeDtypeStruct((t, wd), dt))
        start += nt
    return pl.pallas_call(
        functools.partial(_inproj_body, seg_tiles=seg_tiles, scaled_tiles=scaled_cols // tn, scale=scale),
        grid=(t // tm, n // tn),
        in_specs=[
            pl.BlockSpec((tm, d), lambda i, j: (i, 0)),
            pl.BlockSpec((None, d, tn), lambda i, j: (layer, 0, j)),
        ],
        out_specs=out_specs,
        out_shape=out_shapes,
        compiler_params=_cparams(("parallel", "arbitrary")),
        name="inproj",
    )(h, w_bf16)


def _outproj_body(ya_ref, yb_ref, wa_ref, wb_ref, x_ref, *rest):
    x = x_ref[...] + _dot(ya_ref[...], wa_ref[...]) + _dot(yb_ref[...], wb_ref[...])
    if len(rest) == 1:
        rest[0][...] = x
    else:
        ln_ref, o_ref, h_ref = rest
        o_ref[...] = x
        h_ref[...] = _rmsnorm_bf16(x, ln_ref[...])


def _outproj(ya, yb, a_col, b_col, w_bf16, layer, x2, next_ln, tm):
    t, d = x2.shape
    half = w_bf16.shape[1] // 2
    row_tile = pl.BlockSpec((tm, d), lambda i: (i, 0))
    in_specs = [
        pl.BlockSpec((tm, half), lambda i: (i, a_col)),
        pl.BlockSpec((tm, half), lambda i: (i, b_col)),
        pl.BlockSpec((None, half, d), lambda i: (layer, 0, 0)),
        pl.BlockSpec((None, half, d), lambda i: (layer, 1, 0)),
        row_tile,
    ]
    args = [ya, yb, w_bf16, w_bf16, x2]
    out_specs, out_shape = row_tile, jax.ShapeDtypeStruct((t, d), F32)
    if next_ln is not None:
        in_specs.append(pl.BlockSpec((1, d), lambda i: (0, 0)))
        args.append(next_ln.reshape(1, d))
        out_specs, out_shape = [row_tile, row_tile], [out_shape, jax.ShapeDtypeStruct((t, d), BF16)]
    out = pl.pallas_call(
        _outproj_body,
        grid=(t // tm,),
        in_specs=in_specs,
        out_specs=out_specs,
        out_shape=out_shape,
        compiler_params=_cparams(("parallel",)),
        name="outproj_residual",
    )(*args)
    return (out, None) if next_ln is None else tuple(out)


def _swa_body(sink_ref, q_ref, g_ref, kvp_ref, kvc_ref, cosp_ref, sinp_ref, cosc_ref, sinc_ref,
              qn_ref, kn_ref, grp_ref, o_ref):
    i = pl.program_id(1)
    blk = q_ref.shape[0]
    lane = lax.broadcasted_iota(jnp.int32, (blk, LANES), 1)
    first_half = (lane // A_HALF) % 2 == 0
    low_head = lane < A_HEAD_DIM
    row = lax.broadcasted_iota(jnp.int32, (blk, 2 * blk), 0)
    col = lax.broadcasted_iota(jnp.int32, (blk, 2 * blk), 1)
    no_prev = jnp.where(i > 0, 0, 2 * blk)
    valid = ((col < blk) & (col > row + no_prev)) | ((col >= blk) & (col - blk <= row))
    cos_c, sin_c = cosc_ref[...], sinc_ref[...]
    grp = grp_ref[...]

    def norm_rope(x, gain, cos, sin):
        ms = _dot(jnp.concatenate(_split_bf16(x * x), axis=1), grp)
        x = x * lax.rsqrt(ms + EPS) * gain
        partner = jnp.where(first_half, pltpu.roll(x, LANES - A_HALF, axis=1), pltpu.roll(x, A_HALF, axis=1))
        return x * cos + partner * sin

    def both_halves(x):
        swapped = pltpu.roll(x, A_HEAD_DIM, axis=1)
        return jnp.where(low_head, x, swapped), jnp.where(low_head, swapped, x)

    def keys(kv_ref, cos, sin):
        k = both_halves(norm_rope(kv_ref[:, :LANES], kn_ref[...], cos, sin))
        v = both_halves(kv_ref[:, LANES:])
        return [(kg.astype(BF16), vg.astype(BF16)) for kg, vg in zip(k, v)]

    prev_kv = keys(kvp_ref, cosp_ref[...], sinp_ref[...])
    cur_kv = keys(kvc_ref, cos_c, sin_c)
    band = [(jnp.concatenate([kp, kc], axis=0), jnp.concatenate([vp, vc], axis=0))
            for (kp, vp), (kc, vc) in zip(prev_kv, cur_kv)]

    heads = range(A_HEADS)
    qs = [norm_rope(q_ref[:, p * LANES:(p + 1) * LANES], qn_ref[...], cos_c, sin_c) for p in range(A_PAIRS)]
    qh = [(jnp.where(low_head, qs[h // 2], 0.0) if h % 2 == 0 else jnp.where(low_head, 0.0, qs[h // 2])).astype(BF16)
          for h in heads]
    s = [jnp.where(valid, _dot_nt(qh[h], band[h // A_GROUP][0]), NEG_BIG) for h in heads]
    sink2 = [sink_ref[h] * LOG2E for h in heads]
    m = [jnp.maximum(jnp.max(s[h], axis=-1, keepdims=True), sink2[h]) for h in heads]
    e = [jnp.exp2(s[h] - m[h]) for h in heads]
    denom = [jnp.sum(e[h], axis=-1, keepdims=True) + jnp.exp2(sink2[h] - m[h]) for h in heads]
    outs = [_dot(e[h].astype(BF16), band[h // A_GROUP][1]) / denom[h] for h in heads]
    for p in range(A_PAIRS):
        o = jnp.where(low_head, outs[2 * p], outs[2 * p + 1])
        o_ref[:, p * LANES:(p + 1) * LANES] = (o * _silu(g_ref[:, p * LANES:(p + 1) * LANES])).astype(o_ref.dtype)


def _hgrn_body(q_ref, f_ref, v_ref, g_ref, lbp_ref, gn_ref, cum_ref, o_ref, st_ref, *, layer,
               before_matmuls=None):
    @pl.when(pl.program_id(1) == 0)
    def _():
        st_ref[...] = jnp.zeros_like(st_ref)

    lbp = lbp_ref[...]
    ex = jnp.exp(lbp - jnp.max(lbp, axis=0, keepdims=True))
    sm = ex / jnp.sum(ex, axis=0, keepdims=True)
    lb = jnp.sum(sm[:layer + 1], axis=0, keepdims=True) - sm[0:1]
    lbc = jnp.maximum(lb, LB_FLOOR)
    oml = 1.0 - lb

    s_pos, s_neg = _sigmoid_pair(f_ref[...])
    logf = jnp.log(lbc + oml * s_pos)
    kin = oml * s_neg - (lbc - lb)
    q = _silu(q_ref[...])

    sums = _dot(cum_ref[...], jnp.concatenate(_split_bf16(logf * LOG2E), axis=0))
    blk = q.shape[0]
    seg = lambda n: sums[n * blk:(n + 1) * blk]
    b_incl, b_rest = seg(B_LEVELS), seg(B_LEVELS + 1)
    b_last = b_incl[blk - 1:blk]

    def level_operand(lv):
        size = 1 << lv
        if size % SUBLANES == 0:
            x = jnp.concatenate([(q if n % 2 else kin)[n * size:(n + 1) * size] for n in range(blk // size)], axis=0)
        else:
            odd = ((lax.broadcasted_iota(jnp.int32, (1, SUBLANES, q.shape[1]), 1) >> lv) & 1) == 1
            grouped = (blk // SUBLANES, SUBLANES, q.shape[1])
            x = jnp.where(odd, q.reshape(grouped), kin.reshape(grouped)).reshape(q.shape)
        return (x * jnp.exp2(seg(lv))).astype(BF16)

    xs = [level_operand(lv) for lv in range(B_LEVELS)]
    q_bf, k_bf = q.astype(BF16), kin.astype(BF16)
    q_state = (q * jnp.exp2(b_incl)).astype(BF16)
    k_state = (kin * jnp.exp2(b_rest)).astype(BF16)
    decay = jnp.exp2(b_last)
    if before_matmuls is not None:
        before_matmuls()

    ri = lax.broadcasted_iota(jnp.int32, (blk, blk), 0)
    ci = lax.broadcasted_iota(jnp.int32, (blk, blk), 1)
    at_level = [(((ri >> lv) & 1) == 1) & ((ci >> lv) == (ri >> lv) - 1) for lv in range(B_LEVELS)] + [ri == ci]
    v = v_ref[...]
    heads = range(B_HEADS)
    hs = [slice(h * B_HEAD_DIM, (h + 1) * B_HEAD_DIM) for h in heads]
    att = [jnp.where(at_level[B_LEVELS], _dot_nt(q_bf[:, s], k_bf[:, s]), 0.0) for s in hs]
    for n in range(B_LEVELS):
        att = [jnp.where(at_level[n], _dot_nt(xs[n][:, s], xs[n][:, s]), a) for s, a in zip(hs, att)]
    st = [st_ref[h] for h in heads]
    o = [_dot(att[h].astype(BF16), v[:, hs[h]].astype(BF16)) + _dot_nt(q_state[:, hs[h]], st[h].astype(BF16))
         for h in heads]
    for h in heads:
        st_ref[h] = st[h] * decay[:, hs[h]] + _dot(v[:, hs[h]].T.astype(BF16), k_state[:, hs[h]])
    ms = [jnp.mean(o[h] * o[h], axis=-1, keepdims=True) for h in heads]
    for h in heads:
        y = o[h] * lax.rsqrt(ms[h] + EPS) * gn_ref[...] * _silu(g_ref[:, hs[h]])
        o_ref[:, hs[h]] = y.astype(o_ref.dtype)


def _hgrn_constants():
    blk = B_BLOCK
    i = np.arange(blk)[:, None]
    j = np.arange(blk)[None, :]
    mats = []
    for lv in range(B_LEVELS):
        s = 1 << lv
        same = (i // s) == (j // s)
        odd = (i // s) % 2 == 1
        mats.append(same & np.where(odd, j <= i, j > i))
    mats += [j <= i, j > i]
    cum = np.concatenate(mats, axis=0).astype(np.float32)
    return np.concatenate([cum, cum], axis=1)


N_SWA_IN = 12
N_HGRN_IN = 7


def _even_mixers_body(*refs, layer):
    swa_in, rest = refs[:N_SWA_IN], refs[N_SWA_IN:]
    hgrn_in, (ya_ref, yb_ref, st_ref) = rest[:N_HGRN_IN], rest[N_HGRN_IN:]
    _hgrn_body(*hgrn_in, yb_ref, st_ref, layer=layer, before_matmuls=lambda: _swa_body(*swa_in, ya_ref))


def _even_mixers(qkv, rest_proj, sinks, cos_t, sin_t, qn_l, kn_l, grp, lower_bounds, gn, batch, seq, layer):
    assert A_BLOCK == B_BLOCK
    nb = seq // A_BLOCK
    kv_w = 2 * A_KV_HEADS * A_HEAD_DIM
    kv_col = A_WIDTH // kv_w
    row = lambda b, i: b * nb + i
    prev = lambda i: jnp.maximum(i - 1, 0)
    cum, gn2 = jnp.asarray(_hgrn_constants(), BF16), gn.reshape(1, B_HEAD_DIM)
    wide = lambda c: pl.BlockSpec((A_BLOCK, A_WIDTH), lambda b, i: (row(b, i), c))
    table = lambda at: pl.BlockSpec((A_BLOCK, LANES), lambda b, i: (at(i), 0))
    whole = lambda a: pl.BlockSpec(a.shape, lambda b, i: (0,) * a.ndim)
    swa_specs = [
        pl.BlockSpec(memory_space=pltpu.SMEM), wide(0), wide(0),
        pl.BlockSpec((A_BLOCK, kv_w), lambda b, i: (row(b, prev(i)), kv_col)),
        pl.BlockSpec((A_BLOCK, kv_w), lambda b, i: (row(b, i), kv_col)),
        table(prev), table(prev), table(lambda i: i), table(lambda i: i),
        whole(qn_l), whole(kn_l), whole(grp),
    ]
    assert len(swa_specs) == N_SWA_IN
    hgrn_specs = [wide(1), wide(2), wide(3), wide(4), whole(lower_bounds), whole(gn2), whole(cum)]
    assert len(hgrn_specs) == N_HGRN_IN
    out = jax.ShapeDtypeStruct((batch * seq, A_WIDTH), BF16)
    return pl.pallas_call(
        functools.partial(_even_mixers_body, layer=layer),
        grid=(batch, nb),
        in_specs=swa_specs + hgrn_specs,
        out_specs=[wide(0), wide(0)],
        out_shape=[out, out],
        scratch_shapes=[pltpu.VMEM((B_HEADS, B_HEAD_DIM, B_HEAD_DIM), F32)],
        compiler_params=_cparams(("parallel", "arbitrary")),
        name="even_mixers",
    )(sinks, qkv, rest_proj, qkv, qkv, cos_t, sin_t, cos_t, sin_t, qn_l, kn_l, grp,
      rest_proj, rest_proj, rest_proj, rest_proj, lower_bounds, gn2, cum)


def _stick_tiles(qs, ks, vs, masks, tri, carry_of):
    t = qs[0].shape[0]
    keep = lambda x, mask: x if mask is None else jnp.where(mask, x, 0.0)
    z2s = [_dot_nt(q, k_blk) for q, k_blk in zip(qs, ks)]
    softplus = [jnp.maximum(z2, 0.0) + jnp.log(1.0 + jnp.exp2(-jnp.abs(z2))) * LOG2E for z2 in z2s]
    ps = [keep(p, mask) for p, mask in zip(softplus, masks)]
    incl = _dot(jnp.concatenate([p.astype(BF16) for p in ps], axis=0), tri)
    incls = [incl[n * t:(n + 1) * t] for n in range(len(qs))]
    totals = [x[:, 0:1] for x in incls]
    ws = [keep(jnp.exp2(z2 - incls[n] - carry_of(n, totals)), mask).astype(BF16)
          for n, (z2, mask) in enumerate(zip(z2s, masks))]
    return [_dot(w, v_blk) for w, v_blk in zip(ws, vs)], totals


def _stick_body(q_ref, k_ref, v_ref, g_ref, tri_ref, o_ref):
    i = pl.program_id(2)
    t = q_ref.shape[0]
    heads = q_ref.shape[1] // C_HEAD_DIM
    hs = [slice(h * C_HEAD_DIM, (h + 1) * C_HEAD_DIM) for h in range(heads)]
    qs = [q_ref[:, s] for s in hs]
    tri2 = tri_ref[...]
    row = lax.broadcasted_iota(jnp.int32, (t, t), 0)
    col = lax.broadcasted_iota(jnp.int32, (t, t), 1)

    def key_tiles(kb):
        start = pl.multiple_of(kb * t, t)
        return ([k_ref[pl.ds(start, t), s] for s in hs], [v_ref[pl.ds(start, t), s] for s in hs])

    kd, vd = key_tiles(i)
    kp, vp = key_tiles(jnp.maximum(i - 1, 0))
    no_prev = jnp.where(i > 0, 0.0, C_NO_TILE)
    outs, totals = _stick_tiles(
        qs + qs, kd + kp, vd + vp, [col < row] * heads + [None] * heads, tri2,
        lambda n, tot: 0.0 if n < heads else tot[n - heads] + no_prev)
    accs = tuple(outs[h] + outs[heads + h] for h in range(heads))
    carries = tuple(totals[h] + totals[heads + h] for h in range(heads))

    def lowest(carries):
        return functools.reduce(jnp.minimum, [jnp.min(c) for c in carries])

    def more(state):
        n, _, _, min_carry = state
        return (n < i) & (min_carry < C_DONE_BITS)

    def step(state):
        n, accs, carries, _ = state
        kn, vn = key_tiles(i - 1 - n)
        outs, totals = _stick_tiles(qs, kn, vn, [None] * heads, tri2, lambda h, tot: carries[h])
        accs = tuple(a + o for a, o in zip(accs, outs))
        carries = tuple(c + tot for c, tot in zip(carries, totals))
        return n + 1, accs, carries, lowest(carries)

    _, accs, _, _ = lax.while_loop(more, step, (jnp.int32(1), accs, carries, lowest(carries)))
    for h in range(heads):
        o_ref[:, hs[h]] = (accs[h] * _silu(g_ref[:, hs[h]])).astype(o_ref.dtype)


def _stick(qkv, gate, batch, seq):
    t = C_TILE
    nq = seq // t
    hw = C_HEADS_PER_STEP * C_HEAD_DIM
    groups = C_HEADS // C_HEADS_PER_STEP
    tri = jnp.asarray(np.arange(t)[:, None] >= np.arange(t)[None, :], BF16)
    row = lambda b, h, i: b * nq + i
    return pl.pallas_call(
        _stick_body,
        grid=(batch, groups, nq),
        in_specs=[
            pl.BlockSpec((t, hw), lambda b, h, i: (row(b, h, i), h)),
            pl.BlockSpec((seq, hw), lambda b, h, i: (b, groups + h)),
            pl.BlockSpec((seq, hw), lambda b, h, i: (b, 2 * groups + h)),
            pl.BlockSpec((t, hw), lambda b, h, i: (row(b, h, i), h)),
            pl.BlockSpec((t, t), lambda b, h, i: (0, 0)),
        ],
        out_specs=pl.BlockSpec((t, hw), lambda b, h, i: (row(b, h, i), h)),
        out_shape=jax.ShapeDtypeStruct((batch * seq, C_WIDTH), BF16),
        compiler_params=_cparams(("parallel", "parallel", "arbitrary")),
        name="stick_breaking_mixer",
    )(qkv, qkv, qkv, gate, tri)


def _swa_constants():
    lane = np.arange(LANES)
    grp = (lane[:, None] // A_HEAD_DIM == lane[None, :] // A_HEAD_DIM).astype(np.float32) / A_HEAD_DIM
    return np.concatenate([grp, grp], axis=0)


def _rope_tables(seq):
    half = A_HALF
    inv_freq = 1.0 / (ROPE_THETA ** (jnp.arange(half, dtype=F32) / half))
    ang = jnp.arange(seq, dtype=F32)[:, None] * inv_freq[None, :]
    cos, sin = jnp.cos(ang), jnp.sin(ang)
    return jnp.tile(cos, (1, 4)), jnp.concatenate([-sin, sin, -sin, sin], axis=1)


def _even_layer(x2, h, batch, seq, w_in, qn, kn, sinks, lower_bounds, gn, w_out, layer, tables, next_ln):
    a_cols = A_WIDTH + 2 * A_KV_HEADS * A_HEAD_DIM
    rest = w_in.shape[2] - a_cols
    qkv, rest_proj = _inproj(h, w_in, layer, [(a_cols, F32), (rest, F32)], tm=1024, tn=a_cols)
    qn_l = (jnp.tile(qn, 2) * (A_HEAD_DIM ** -0.5 * LOG2E)).reshape(1, LANES)
    kn_l = jnp.tile(kn, 2).reshape(1, LANES)
    ya, yb = _even_mixers(qkv, rest_proj, sinks, tables[0], tables[1], qn_l, kn_l,
                          jnp.asarray(_swa_constants(), BF16), lower_bounds, gn, batch, seq, layer)
    return _outproj(ya, yb, 0, 0, w_out, layer, x2, next_ln, tm=512)


def _odd_layer(x2, h, batch, seq, w_in, w_out, layer, next_ln):
    qkv, gate = _inproj(h, w_in, layer, [(3 * C_WIDTH, BF16), (C_WIDTH, F32)],
                        tm=512, tn=C_WIDTH, scaled_cols=C_WIDTH, scale=C_HEAD_DIM ** -0.5 * LOG2E)
    y = _stick(qkv, gate, batch, seq)
    return _outproj(y, y, 0, 1, w_out, layer, x2, next_ln, tm=512)


def kernel(x, ln_even, w_in_even, q_norm_a, k_norm_a, sinks_a, lower_bounds, g_norm_b, w_out_even,
           ln_odd, w_in_odd, w_out_odd):
    batch, seq, d = x.shape
    depth = ln_even.shape[0] + ln_odd.shape[0]
    x2 = x.reshape(batch * seq, d)
    tables = _rope_tables(seq)
    w_in_even, w_out_even = w_in_even.astype(BF16), w_out_even.astype(BF16)
    w_in_odd, w_out_odd = w_in_odd.astype(BF16), w_out_odd.astype(BF16)
    pre_norm = lambda layer: (ln_even if layer % 2 == 0 else ln_odd)[layer // 2]
    h = _norm(x2, pre_norm(0), tm=512)
    for layer in range(depth):
        e = layer // 2
        next_ln = pre_norm(layer + 1) if layer + 1 < depth else None
        if layer % 2 == 0:
            x2, h = _even_layer(x2, h, batch, seq, w_in_even, q_norm_a[e], k_norm_a[e], sinks_a[e],
                                lower_bounds, g_norm_b[e], w_out_even, e, tables, next_ln)
        else:
            x2, h = _odd_layer(x2, h, batch, seq, w_in_odd, w_out_odd, e, next_ln)
    return x2.reshape(batch, seq, d)
```

```python
import functools
import math

import jax
import jax.numpy as jnp
import numpy as np
from jax import lax
from jax.experimental import pallas as pl
from jax.experimental.pallas import tpu as pltpu

F32 = jnp.float32
BF16 = jnp.bfloat16

EPS = 1e-6
ROPE_THETA = 10000.0
LB_FLOOR = 1e-30
NEG_BIG = -1e30

V7X_VMEM_BYTES = 64 * 1024 * 1024
VMEM_LIMIT = V7X_VMEM_BYTES - 8 * 1024 * 1024
LANES = 128
SUBLANES = 8
INPROJ_CHUNK = 512

A_HEAD_DIM = 64
A_HALF = A_HEAD_DIM // 2
A_HEADS = 16
A_KV_HEADS = 2
A_GROUP = A_HEADS // A_KV_HEADS
A_PAIRS = A_HEADS // 2
A_WIDTH = A_HEADS * A_HEAD_DIM
A_BLOCK = 128
B_HEAD_DIM = 128
B_HEADS = 8
B_WIDTH = B_HEADS * B_HEAD_DIM
B_BLOCK = 128
B_LEVELS = int(math.log2(B_BLOCK))
C_HEAD_DIM = 128
C_HEADS = 16
C_WIDTH = C_HEADS * C_HEAD_DIM
C_TILE = 256
C_HEADS_PER_STEP = 4
LOG2E = 1.0 / math.log(2.0)
C_DONE_BITS = 150.0
C_NO_TILE = 1e30


def _cparams(semantics):
    return pltpu.CompilerParams(dimension_semantics=semantics, vmem_limit_bytes=VMEM_LIMIT)


def _split_bf16(x):
    hi = x.astype(BF16)
    lo = (x - hi.astype(F32)).astype(BF16)
    return hi, lo


def _dot(a, b):
    return jnp.dot(a, b, preferred_element_type=F32)


def _dot_nt(a, b):
    return lax.dot_general(a, b, (((1,), (1,)), ((), ())), preferred_element_type=F32)


def _sigmoid_pair(x):
    half_t = 0.5 * jnp.tanh(0.5 * x)
    return 0.5 + half_t, 0.5 - half_t


def _silu(x):
    return x * _sigmoid_pair(x)[0]


def _rmsnorm_bf16(x, gain):
    ms = jnp.mean(x * x, axis=-1, keepdims=True)
    return (x * lax.rsqrt(ms + EPS) * gain).astype(BF16)


def _norm_body(x_ref, ln_ref, h_ref):
    h_ref[...] = _rmsnorm_bf16(x_ref[...], ln_ref[...])


def _norm(x2, ln, tm):
    t, d = x2.shape
    return pl.pallas_call(
        _norm_body,
        grid=(t // tm,),
        in_specs=[pl.BlockSpec((tm, d), lambda i: (i, 0)), pl.BlockSpec((1, d), lambda i: (0, 0))],
        out_specs=pl.BlockSpec((tm, d), lambda i: (i, 0)),
        out_shape=jax.ShapeDtypeStruct((t, d), BF16),
        compiler_params=_cparams(("parallel",)),
        name="rmsnorm",
    )(x2, ln.reshape(1, d))


def _inproj_body(h_ref, w_ref, *out_refs, seg_tiles, scaled_tiles, scale):
    j = pl.program_id(1)
    tn = w_ref.shape[1]
    chunk = math.gcd(tn, INPROJ_CHUNK)
    start = 0
    for o_ref, nt in zip(out_refs, seg_tiles):
        @pl.when((j >= start) & (j < start + nt))
        def _(o_ref=o_ref):
            for c in range(0, tn, chunk):
                acc = _dot(h_ref[...], w_ref[:, c:c + chunk])
                if scaled_tiles:
                    acc = acc * jnp.where(j < scaled_tiles, scale, 1.0)
                o_ref[:, c:c + chunk] = acc.astype(o_ref.dtype)
        start += nt


def _inproj(h, w_bf16, layer, segs, tm, tn, scaled_cols=0, scale=1.0):
    t, d = h.shape
    n = w_bf16.shape[2]
    seg_tiles = tuple(wd // tn for wd, _ in segs)
    assert sum(seg_tiles) * tn == n and t % tm == 0 and scaled_cols % tn == 0
    out_specs, out_shapes, start = [], [], 0
    for (wd, dt), nt in zip(segs, seg_tiles):
        out_specs.append(pl.BlockSpec(
            (tm, tn), lambda i, j, s=start, m=nt - 1: (i, jnp.clip(j - s, 0, m))))
        out_shapes.append(jax.ShapeDtypeStruct((t, wd), dt))
        start += nt
    return pl.pallas_call(
        functools.partial(_inproj_body, seg_tiles=seg_tiles, scaled_tiles=scaled_cols // tn, scale=scale),
        grid=(t // tm, n // tn),
        in_specs=[
            pl.BlockSpec((tm, d), lambda i, j: (i, 0)),
            pl.BlockSpec((None, d, tn), lambda i, j: (layer, 0, j)),
        ],
        out_specs=out_specs,
        out_shape=out_shapes,
        compiler_params=_cparams(("parallel", "arbitrary")),
        name="inproj",
    )(h, w_bf16)


def _outproj_body(ya_ref, yb_ref, wa_ref, wb_ref, x_ref, *rest):
    x = x_ref[...] + _dot(ya_ref[...], wa_ref[...]) + _dot(yb_ref[...], wb_ref[...])
    if len(rest) == 1:
        rest[0][...] = x
    else:
        ln_ref, o_ref, h_ref = rest
        o_ref[...] = x
        h_ref[...] = _rmsnorm_bf16(x, ln_ref[...])


def _outproj(ya, yb, a_col, b_col, w_bf16, layer, x2, next_ln, tm):
    t, d = x2.shape
    half = w_bf16.shape[1] // 2
    row_tile = pl.BlockSpec((tm, d), lambda i: (i, 0))
    in_specs = [
        pl.BlockSpec((tm, half), lambda i: (i, a_col)),
        pl.BlockSpec((tm, half), lambda i: (i, b_col)),
        pl.BlockSpec((None, half, d), lambda i: (layer, 0, 0)),
        pl.BlockSpec((None, half, d), lambda i: (layer, 1, 0)),
        row_tile,
    ]
    args = [ya, yb, w_bf16, w_bf16, x2]
    out_specs, out_shape = row_tile, jax.ShapeDtypeStruct((t, d), F32)
    if next_ln is not None:
        in_specs.append(pl.BlockSpec((1, d), lambda i: (0, 0)))
        args.append(next_ln.reshape(1, d))
        out_specs, out_shape = [row_tile, row_tile], [out_shape, jax.ShapeDtypeStruct((t, d), BF16)]
    out = pl.pallas_call(
        _outproj_body,
        grid=(t // tm,),
        in_specs=in_specs,
        out_specs=out_specs,
        out_shape=out_shape,
        compiler_params=_cparams(("parallel",)),
        name="outproj_residual",
    )(*args)
    return (out, None) if next_ln is None else tuple(out)


def _swa_body(sink_ref, q_ref, g_ref, kvp_ref, kvc_ref, cosp_ref, sinp_ref, cosc_ref, sinc_ref,
              qn_ref, kn_ref, grp_ref, o_ref):
    i = pl.program_id(1)
    blk = q_ref.shape[0]
    lane = lax.broadcasted_iota(jnp.int32, (blk, LANES), 1)
    first_half = (lane // A_HALF) % 2 == 0
    low_head = lane < A_HEAD_DIM
    row = lax.broadcasted_iota(jnp.int32, (blk, 2 * blk), 0)
    col = lax.broadcasted_iota(jnp.int32, (blk, 2 * blk), 1)
    no_prev = jnp.where(i > 0, 0, 2 * blk)
    valid = ((col < blk) & (col > row + no_prev)) | ((col >= blk) & (col - blk <= row))
    cos_c, sin_c = cosc_ref[...], sinc_ref[...]
    grp = grp_ref[...]

    def norm_rope(x, gain, cos, sin):
        ms = _dot(jnp.concatenate(_split_bf16(x * x), axis=1), grp)
        x = x * lax.rsqrt(ms + EPS) * gain
        partner = jnp.where(first_half, pltpu.roll(x, LANES - A_HALF, axis=1), pltpu.roll(x, A_HALF, axis=1))
        return x * cos + partner * sin

    def both_halves(x):
        swapped = pltpu.roll(x, A_HEAD_DIM, axis=1)
        return jnp.where(low_head, x, swapped), jnp.where(low_head, swapped, x)

    def keys(kv_ref, cos, sin):
        k = both_halves(norm_rope(kv_ref[:, :LANES], kn_ref[...], cos, sin))
        v = both_halves(kv_ref[:, LANES:])
        return [(kg.astype(BF16), vg.astype(BF16)) for kg, vg in zip(k, v)]

    prev_kv = keys(kvp_ref, cosp_ref[...], sinp_ref[...])
    cur_kv = keys(kvc_ref, cos_c, sin_c)
    band = [(jnp.concatenate([kp, kc], axis=0), jnp.concatenate([vp, vc], axis=0))
            for (kp, vp), (kc, vc) in zip(prev_kv, cur_kv)]

    heads = range(A_HEADS)
    qs = [norm_rope(q_ref[:, p * LANES:(p + 1) * LANES], qn_ref[...], cos_c, sin_c) for p in range(A_PAIRS)]
    qh = [(jnp.where(low_head, qs[h // 2], 0.0) if h % 2 == 0 else jnp.where(low_head, 0.0, qs[h // 2])).astype(BF16)
          for h in heads]
    s = [jnp.where(valid, _dot_nt(qh[h], band[h // A_GROUP][0]), NEG_BIG) for h in heads]
    sink2 = [sink_ref[h] * LOG2E for h in heads]
    m = [jnp.maximum(jnp.max(s[h], axis=-1, keepdims=True), sink2[h]) for h in heads]
    e = [jnp.exp2(s[h] - m[h]) for h in heads]
    denom = [jnp.sum(e[h], axis=-1, keepdims=True) + jnp.exp2(sink2[h] - m[h]) for h in heads]
    outs = [_dot(e[h].astype(BF16), band[h // A_GROUP][1]) / denom[h] for h in heads]
    for p in range(A_PAIRS):
        o = jnp.where(low_head, outs[2 * p], outs[2 * p + 1])
        o_ref[:, p * LANES:(p + 1) * LANES] = (o * _silu(g_ref[:, p * LANES:(p + 1) * LANES])).astype(o_ref.dtype)


def _hgrn_body(q_ref, f_ref, v_ref, g_ref, lbp_ref, gn_ref, cum_ref, o_ref, st_ref, *, layer,
               before_matmuls=None):
    @pl.when(pl.program_id(1) == 0)
    def _():
        st_ref[...] = jnp.zeros_like(st_ref)

    lbp = lbp_ref[...]
    ex = jnp.exp(lbp - jnp.max(lbp, axis=0, keepdims=True))
    sm = ex / jnp.sum(ex, axis=0, keepdims=True)
    lb = jnp.sum(sm[:layer + 1], axis=0, keepdims=True) - sm[0:1]
    lbc = jnp.maximum(lb, LB_FLOOR)
    oml = 1.0 - lb

    s_pos, s_neg = _sigmoid_pair(f_ref[...])
    logf = jnp.log(lbc + oml * s_pos)
    kin = oml * s_neg - (lbc - lb)
    q = _silu(q_ref[...])

    sums = _dot(cum_ref[...], jnp.concatenate(_split_bf16(logf * LOG2E), axis=0))
    blk = q.shape[0]
    seg = lambda n: sums[n * blk:(n + 1) * blk]
    b_incl, b_rest = seg(B_LEVELS), seg(B_LEVELS + 1)
    b_last = b_incl[blk - 1:blk]

    def level_operand(lv):
        size = 1 << lv
        if size % SUBLANES == 0:
            x = jnp.concatenate([(q if n % 2 else kin)[n * size:(n + 1) * size] for n in range(blk // size)], axis=0)
        else:
            odd = ((lax.broadcasted_iota(jnp.int32, (1, SUBLANES, q.shape[1]), 1) >> lv) & 1) == 1
            grouped = (blk // SUBLANES, SUBLANES, q.shape[1])
            x = jnp.where(odd, q.reshape(grouped), kin.reshape(grouped)).reshape(q.shape)
        return (x * jnp.exp2(seg(lv))).astype(BF16)

    xs = [level_operand(lv) for lv in range(B_LEVELS)]
    q_bf, k_bf = q.astype(BF16), kin.astype(BF16)
    q_state = (q * jnp.exp2(b_incl)).astype(BF16)
    k_state = (kin * jnp.exp2(b_rest)).astype(BF16)
    decay = jnp.exp2(b_last)
    if before_matmuls is not None:
        before_matmuls()

    ri = lax.broadcasted_iota(jnp.int32, (blk, blk), 0)
    ci = lax.broadcasted_iota(jnp.int32, (blk, blk), 1)
    at_level = [(((ri >> lv) & 1) == 1) & ((ci >> lv) == (ri >> lv) - 1) for lv in range(B_LEVELS)] + [ri == ci]
    v = v_ref[...]
    heads = range(B_HEADS)
    hs = [slice(h * B_HEAD_DIM, (h + 1) * B_HEAD_DIM) for h in heads]
    att = [jnp.where(at_level[B_LEVELS], _dot_nt(q_bf[:, s], k_bf[:, s]), 0.0) for s in hs]
    for n in range(B_LEVELS):
        att = [jnp.where(at_level[n], _dot_nt(xs[n][:, s], xs[n][:, s]), a) for s, a in zip(hs, att)]
    st = [st_ref[h] for h in heads]
    o = [_dot(att[h].astype(BF16), v[:, hs[h]].astype(BF16)) + _dot_nt(q_state[:, hs[h]], st[h].astype(BF16))
         for h in heads]
    for h in heads:
        st_ref[h] = st[h] * decay[:, hs[h]] + _dot(v[:, hs[h]].T.astype(BF16), k_state[:, hs[h]])
    ms = [jnp.mean(o[h] * o[h], axis=-1, keepdims=True) for h in heads]
    for h in heads:
        y = o[h] * lax.rsqrt(ms[h] + EPS) * gn_ref[...] * _silu(g_ref[:, hs[h]])
        o_ref[:, hs[h]] = y.astype(o_ref.dtype)


def _hgrn_constants():
    blk = B_BLOCK
    i = np.arange(blk)[:, None]
    j = np.arange(blk)[None, :]
    mats = []
    for lv in range(B_LEVELS):
        s = 1 << lv
        same = (i // s) == (j // s)
        odd = (i // s) % 2 == 1
        mats.append(same & np.where(odd, j <= i, j > i))
    mats += [j <= i, j > i]
    cum = np.concatenate(mats, axis=0).astype(np.float32)
    return np.concatenate([cum, cum], axis=1)


N_SWA_IN = 12
N_HGRN_IN = 7


def _even_mixers_body(*refs, layer):
    swa_in, rest = refs[:N_SWA_IN], refs[N_SWA_IN:]
    hgrn_in, (ya_ref, yb_ref, st_ref) = rest[:N_HGRN_IN], rest[N_HGRN_IN:]
    _hgrn_body(*hgrn_in, yb_ref, st_ref, layer=layer, before_matmuls=lambda: _swa_body(*swa_in, ya_ref))


def _even_mixers(qkv, rest_proj, sinks, cos_t, sin_t, qn_l, kn_l, grp, lower_bounds, gn, batch, seq, layer):
    assert A_BLOCK == B_BLOCK
    nb = seq // A_BLOCK
    kv_w = 2 * A_KV_HEADS * A_HEAD_DIM
    kv_col = A_WIDTH // kv_w
    row = lambda b, i: b * nb + i
    prev = lambda i: jnp.maximum(i - 1, 0)
    cum, gn2 = jnp.asarray(_hgrn_constants(), BF16), gn.reshape(1, B_HEAD_DIM)
    wide = lambda c: pl.BlockSpec((A_BLOCK, A_WIDTH), lambda b, i: (row(b, i), c))
    table = lambda at: pl.BlockSpec((A_BLOCK, LANES), lambda b, i: (at(i), 0))
    whole = lambda a: pl.BlockSpec(a.shape, lambda b, i: (0,) * a.ndim)
    swa_specs = [
        pl.BlockSpec(memory_space=pltpu.SMEM), wide(0), wide(0),
        pl.BlockSpec((A_BLOCK, kv_w), lambda b, i: (row(b, prev(i)), kv_col)),
        pl.BlockSpec((A_BLOCK, kv_w), lambda b, i: (row(b, i), kv_col)),
        table(prev), table(prev), table(lambda i: i), table(lambda i: i),
        whole(qn_l), whole(kn_l), whole(grp),
    ]
    assert len(swa_specs) == N_SWA_IN
    hgrn_specs = [wide(1), wide(2), wide(3), wide(4), whole(lower_bounds), whole(gn2), whole(cum)]
    assert len(hgrn_specs) == N_HGRN_IN
    out = jax.ShapeDtypeStruct((batch * seq, A_WIDTH), BF16)
    return pl.pallas_call(
        functools.partial(_even_mixers_body, layer=layer),
        grid=(batch, nb),
        in_specs=swa_specs + hgrn_specs,
        out_specs=[wide(0), wide(0)],
        out_shape=[out, out],
        scratch_shapes=[pltpu.VMEM((B_HEADS, B_HEAD_DIM, B_HEAD_DIM), F32)],
        compiler_params=_cparams(("parallel", "arbitrary")),
        name="even_mixers",
    )(sinks, qkv, rest_proj, qkv, qkv, cos_t, sin_t, cos_t, sin_t, qn_l, kn_l, grp,
      rest_proj, rest_proj, rest_proj, rest_proj, lower_bounds, gn2, cum)


def _stick_tiles(qs, ks, vs, masks, tri, carry_of):
    t = qs[0].shape[0]
    keep = lambda x, mask: x if mask is None else jnp.where(mask, x, 0.0)
    z2s = [_dot_nt(q, k_blk) for q, k_blk in zip(qs, ks)]
    softplus = [jnp.maximum(z2, 0.0) + jnp.log(1.0 + jnp.exp2(-jnp.abs(z2))) * LOG2E for z2 in z2s]
    ps = [keep(p, mask) for p, mask in zip(softplus, masks)]
    incl = _dot(jnp.concatenate([p.astype(BF16) for p in ps], axis=0), tri)
    incls = [incl[n * t:(n + 1) * t] for n in range(len(qs))]
    totals = [x[:, 0:1] for x in incls]
    ws = [keep(jnp.exp2(z2 - incls[n] - carry_of(n, totals)), mask).astype(BF16)
          for n, (z2, mask) in enumerate(zip(z2s, masks))]
    return [_dot(w, v_blk) for w, v_blk in zip(ws, vs)], totals


def _stick_body(q_ref, k_ref, v_ref, g_ref, tri_ref, o_ref):
    i = pl.program_id(2)
    t = q_ref.shape[0]
    heads = q_ref.shape[1] // C_HEAD_DIM
    hs = [slice(h * C_HEAD_DIM, (h + 1) * C_HEAD_DIM) for h in range(heads)]
    qs = [q_ref[:, s] for s in hs]
    tri = tri_ref[...]
    gates = [_silu(g_ref[:, s]) for s in hs]
    row = lax.broadcasted_iota(jnp.int32, (t, t), 0)
    col = lax.broadcasted_iota(jnp.int32, (t, t), 1)

    def key_tiles(kb):
        start = pl.multiple_of(kb * t, t)
        return ([k_ref[pl.ds(start, t), s] for s in hs], [v_ref[pl.ds(start, t), s] for s in hs])

    kd, vd = key_tiles(i)
    kp, vp = key_tiles(jnp.maximum(i - 1, 0))
    no_prev = jnp.where(i > 0, 0.0, C_NO_TILE)
    outs, totals = _stick_tiles(
        qs + qs, kd + kp, vd + vp, [col < row] * heads + [None] * heads, tri,
        lambda n, tot: 0.0 if n < heads else tot[n - heads] + no_prev)
    accs = tuple(outs[h] + outs[heads + h] for h in range(heads))
    carries = tuple(totals[h] + totals[heads + h] for h in range(heads))

    def lowest(carries):
        return functools.reduce(jnp.minimum, [jnp.min(c) for c in carries])

    def more(state):
        n, _, _, min_carry = state
        return (n < i) & (min_carry < C_DONE_BITS)

    def step(state):
        n, accs, carries, _ = state
        kn, vn = key_tiles(i - 1 - n)
        outs, totals = _stick_tiles(qs, kn, vn, [None] * heads, tri, lambda h, tot: carries[h])
        accs = tuple(a + o for a, o in zip(accs, outs))
        carries = tuple(c + tot for c, tot in zip(carries, totals))
        return n + 1, accs, carries, lowest(carries)

    _, accs, _, _ = lax.while_loop(more, step, (jnp.int32(1), accs, carries, lowest(carries)))
    for h in range(heads):
        o_ref[:, hs[h]] = (accs[h] * gates[h]).astype(o_ref.dtype)


def _stick(qkv, gate, batch, seq):
    t = C_TILE
    nq = seq // t
    hw = C_HEADS_PER_STEP * C_HEAD_DIM
    groups = C_HEADS // C_HEADS_PER_STEP
    tri = jnp.asarray(np.arange(t)[:, None] >= np.arange(t)[None, :], BF16)
    row = lambda b, h, i: b * nq + i
    return pl.pallas_call(
        _stick_body,
        grid=(batch, groups, nq),
        in_specs=[
            pl.BlockSpec((t, hw), lambda b, h, i: (row(b, h, i), h)),
            pl.BlockSpec((seq, hw), lambda b, h, i: (b, groups + h)),
            pl.BlockSpec((seq, hw), lambda b, h, i: (b, 2 * groups + h)),
            pl.BlockSpec((t, hw), lambda b, h, i: (row(b, h, i), h)),
            pl.BlockSpec((t, t), lambda b, h, i: (0, 0)),
        ],
        out_specs=pl.BlockSpec((t, hw), lambda b, h, i: (row(b, h, i), h)),
        out_shape=jax.ShapeDtypeStruct((batch * seq, C_WIDTH), BF16),
        compiler_params=_cparams(("parallel", "parallel", "arbitrary")),
        name="stick_breaking_mixer",
    )(qkv, qkv, qkv, gate, tri)


def _swa_constants():
    lane = np.arange(LANES)
    grp = (lane[:, None] // A_HEAD_DIM == lane[None, :] // A_HEAD_DIM).astype(np.float32) / A_HEAD_DIM
    return np.concatenate([grp, grp], axis=0)


def _rope_tables(seq):
    half = A_HALF
    inv_freq = 1.0 / (ROPE_THETA ** (jnp.arange(half, dtype=F32) / half))
    ang = jnp.arange(seq, dtype=F32)[:, None] * inv_freq[None, :]
    cos, sin = jnp.cos(ang), jnp.sin(ang)
    return jnp.tile(cos, (1, 4)), jnp.concatenate([-sin, sin, -sin, sin], axis=1)


def _even_layer(x2, h, batch, seq, w_in, qn, kn, sinks, lower_bounds, gn, w_out, layer, tables, next_ln):
    a_cols = A_WIDTH + 2 * A_KV_HEADS * A_HEAD_DIM
    rest = w_in.shape[2] - a_cols
    qkv, rest_proj = _inproj(h, w_in, layer, [(a_cols, F32), (rest, F32)], tm=1024, tn=a_cols)
    qn_l = (jnp.tile(qn, 2) * (A_HEAD_DIM ** -0.5 * LOG2E)).reshape(1, LANES)
    kn_l = jnp.tile(kn, 2).reshape(1, LANES)
    ya, yb = _even_mixers(qkv, rest_proj, sinks, tables[0], tables[1], qn_l, kn_l,
                          jnp.asarray(_swa_constants(), BF16), lower_bounds, gn, batch, seq, layer)
    return _outproj(ya, yb, 0, 0, w_out, layer, x2, next_ln, tm=512)


def _odd_layer(x2, h, batch, seq, w_in, w_out, layer, next_ln):
    qkv, gate = _inproj(h, w_in, layer, [(3 * C_WIDTH, BF16), (C_WIDTH, F32)],
                        tm=1024, tn=C_WIDTH // 2, scaled_cols=C_WIDTH, scale=C_HEAD_DIM ** -0.5 * LOG2E)
    y = _stick(qkv, gate, batch, seq)
    return _outproj(y, y, 0, 1, w_out, layer, x2, next_ln, tm=512)


def kernel(x, ln_even, w_in_even, q_norm_a, k_norm_a, sinks_a, lower_bounds, g_norm_b, w_out_even,
           ln_odd, w_in_odd, w_out_odd):
    batch, seq, d = x.shape
    depth = ln_even.shape[0] + ln_odd.shape[0]
    x2 = x.reshape(batch * seq, d)
    tables = _rope_tables(seq)
    w_in_even, w_out_even = w_in_even.astype(BF16), w_out_even.astype(BF16)
    w_in_odd, w_out_odd = w_in_odd.astype(BF16), w_out_odd.astype(BF16)
    pre_norm = lambda layer: (ln_even if layer % 2 == 0 else ln_odd)[layer // 2]
    h = _norm(x2, pre_norm(0), tm=512)
    for layer in range(depth):
        e = layer // 2
        next_ln = pre_norm(layer + 1) if layer + 1 < depth else None
        if layer % 2 == 0:
            x2, h = _even_layer(x2, h, batch, seq, w_in_even, q_norm_a[e], k_norm_a[e], sinks_a[e],
                                lower_bounds, g_norm_b[e], w_out_even, e, tables, next_ln)
        else:
            x2, h = _odd_layer(x2, h, batch, seq, w_in_odd, w_out_odd, e, next_ln)
    return x2.reshape(batch, seq, d)
```

```python
import functools
import math

import jax
import jax.numpy as jnp
import numpy as np
from jax import lax
from jax.experimental import pallas as pl
from jax.experimental.pallas import tpu as pltpu

F32 = jnp.float32
BF16 = jnp.bfloat16

EPS = 1e-6
ROPE_THETA = 10000.0
LB_FLOOR = 1e-30
NEG_BIG = -1e30

V7X_VMEM_BYTES = 64 * 1024 * 1024
VMEM_LIMIT = V7X_VMEM_BYTES - 8 * 1024 * 1024
LANES = 128
SUBLANES = 8
INPROJ_CHUNK = 512

A_HEAD_DIM = 64
A_HALF = A_HEAD_DIM // 2
A_HEADS = 16
A_KV_HEADS = 2
A_GROUP = A_HEADS // A_KV_HEADS
A_PAIRS = A_HEADS // 2
A_WIDTH = A_HEADS * A_HEAD_DIM
A_BLOCK = 128
B_HEAD_DIM = 128
B_HEADS = 8
B_WIDTH = B_HEADS * B_HEAD_DIM
B_BLOCK = 128
B_LEVELS = int(math.log2(B_BLOCK))
C_HEAD_DIM = 128
C_HEADS = 16
C_WIDTH = C_HEADS * C_HEAD_DIM
C_TILE = 256
C_HEADS_PER_STEP = 4
C_QTILES_PER_STEP = 2
LOG2E = 1.0 / math.log(2.0)
C_DONE_BITS = 150.0
C_NO_TILE = 1e30


def _cparams(semantics):
    return pltpu.CompilerParams(dimension_semantics=semantics, vmem_limit_bytes=VMEM_LIMIT)


def _split_bf16(x):
    hi = x.astype(BF16)
    lo = (x - hi.astype(F32)).astype(BF16)
    return hi, lo


def _dot(a, b):
    return jnp.dot(a, b, preferred_element_type=F32)


def _dot_nt(a, b):
    return lax.dot_general(a, b, (((1,), (1,)), ((), ())), preferred_element_type=F32)


def _sigmoid_pair(x):
    half_t = 0.5 * jnp.tanh(0.5 * x)
    return 0.5 + half_t, 0.5 - half_t


def _silu(x):
    return x * _sigmoid_pair(x)[0]


def _rmsnorm_bf16(x, gain):
    ms = jnp.mean(x * x, axis=-1, keepdims=True)
    return (x * lax.rsqrt(ms + EPS) * gain).astype(BF16)


def _norm_body(x_ref, ln_ref, h_ref):
    h_ref[...] = _rmsnorm_bf16(x_ref[...], ln_ref[...])


def _norm(x2, ln, tm):
    t, d = x2.shape
    return pl.pallas_call(
        _norm_body,
        grid=(t // tm,),
        in_specs=[pl.BlockSpec((tm, d), lambda i: (i, 0)), pl.BlockSpec((1, d), lambda i: (0, 0))],
        out_specs=pl.BlockSpec((tm, d), lambda i: (i, 0)),
        out_shape=jax.ShapeDtypeStruct((t, d), BF16),
        compiler_params=_cparams(("parallel",)),
        name="rmsnorm",
    )(x2, ln.reshape(1, d))


def _inproj_body(h_ref, w_ref, *out_refs, seg_tiles, scaled_tiles, scale):
    j = pl.program_id(1)
    tn = w_ref.shape[1]
    chunk = math.gcd(tn, INPROJ_CHUNK)
    start = 0
    for o_ref, nt in zip(out_refs, seg_tiles):
        @pl.when((j >= start) & (j < start + nt))
        def _(o_ref=o_ref):
            for c in range(0, tn, chunk):
                acc = _dot(h_ref[...], w_ref[:, c:c + chunk])
                if scaled_tiles:
                    acc = acc * jnp.where(j < scaled_tiles, scale, 1.0)
                o_ref[:, c:c + chunk] = acc.astype(o_ref.dtype)
        start += nt


def _inproj(h, w_bf16, layer, segs, tm, tn, scaled_cols=0, scale=1.0):
    t, d = h.shape
    n = w_bf16.shape[2]
    seg_tiles = tuple(wd // tn for wd, _ in segs)
    assert sum(seg_tiles) * tn == n and t % tm == 0 and scaled_cols % tn == 0
    out_specs, out_shapes, start = [], [], 0
    for (wd, dt), nt in zip(segs, seg_tiles):
        out_specs.append(pl.BlockSpec(
            (tm, tn), lambda i, j, s=start, m=nt - 1: (i, jnp.clip(j - s, 0, m))))
        out_shapes.append(jax.ShapeDtypeStruct((t, wd), dt))
        start += nt
    return pl.pallas_call(
        functools.partial(_inproj_body, seg_tiles=seg_tiles, scaled_tiles=scaled_cols // tn, scale=scale),
        grid=(t // tm, n // tn),
        in_specs=[
            pl.BlockSpec((tm, d), lambda i, j: (i, 0)),
            pl.BlockSpec((None, d, tn), lambda i, j: (layer, 0, j)),
        ],
        out_specs=out_specs,
        out_shape=out_shapes,
        compiler_params=_cparams(("parallel", "arbitrary")),
        name="inproj",
    )(h, w_bf16)


def _outproj_body(ya_ref, yb_ref, wa_ref, wb_ref, x_ref, *rest):
    x = x_ref[...] + _dot(ya_ref[...], wa_ref[...]) + _dot(yb_ref[...], wb_ref[...])
    if len(rest) == 1:
        rest[0][...] = x
    else:
        ln_ref, o_ref, h_ref = rest
        o_ref[...] = x
        h_ref[...] = _rmsnorm_bf16(x, ln_ref[...])


def _outproj(ya, yb, a_col, b_col, w_bf16, layer, x2, next_ln, tm):
    t, d = x2.shape
    half = w_bf16.shape[1] // 2
    row_tile = pl.BlockSpec((tm, d), lambda i: (i, 0))
    in_specs = [
        pl.BlockSpec((tm, half), lambda i: (i, a_col)),
        pl.BlockSpec((tm, half), lambda i: (i, b_col)),
        pl.BlockSpec((None, half, d), lambda i: (layer, 0, 0)),
        pl.BlockSpec((None, half, d), lambda i: (layer, 1, 0)),
        row_tile,
    ]
    args = [ya, yb, w_bf16, w_bf16, x2]
    out_specs, out_shape = row_tile, jax.ShapeDtypeStruct((t, d), F32)
    if next_ln is not None:
        in_specs.append(pl.BlockSpec((1, d), lambda i: (0, 0)))
        args.append(next_ln.reshape(1, d))
        out_specs, out_shape = [row_tile, row_tile], [out_shape, jax.ShapeDtypeStruct((t, d), BF16)]
    out = pl.pallas_call(
        _outproj_body,
        grid=(t // tm,),
        in_specs=in_specs,
        out_specs=out_specs,
        out_shape=out_shape,
        compiler_params=_cparams(("parallel",)),
        name="outproj_residual",
    )(*args)
    return (out, None) if next_ln is None else tuple(out)


def _swa_body(sink_ref, q_ref, g_ref, kvp_ref, kvc_ref, cosp_ref, sinp_ref, cosc_ref, sinc_ref,
              qn_ref, kn_ref, grp_ref, o_ref):
    i = pl.program_id(1)
    blk = q_ref.shape[0]
    lane = lax.broadcasted_iota(jnp.int32, (blk, LANES), 1)
    first_half = (lane // A_HALF) % 2 == 0
    low_head = lane < A_HEAD_DIM
    row = lax.broadcasted_iota(jnp.int32, (blk, 2 * blk), 0)
    col = lax.broadcasted_iota(jnp.int32, (blk, 2 * blk), 1)
    no_prev = jnp.where(i > 0, 0, 2 * blk)
    valid = ((col < blk) & (col > row + no_prev)) | ((col >= blk) & (col - blk <= row))
    cos_c, sin_c = cosc_ref[...], sinc_ref[...]
    grp = grp_ref[...]

    def norm_rope(x, gain, cos, sin):
        ms = _dot(jnp.concatenate(_split_bf16(x * x), axis=1), grp)
        x = x * lax.rsqrt(ms + EPS) * gain
        partner = jnp.where(first_half, pltpu.roll(x, LANES - A_HALF, axis=1), pltpu.roll(x, A_HALF, axis=1))
        return x * cos + partner * sin

    def both_halves(x):
        swapped = pltpu.roll(x, A_HEAD_DIM, axis=1)
        return jnp.where(low_head, x, swapped), jnp.where(low_head, swapped, x)

    def keys(kv_ref, cos, sin):
        k = both_halves(norm_rope(kv_ref[:, :LANES], kn_ref[...], cos, sin))
        v = both_halves(kv_ref[:, LANES:])
        return [(kg.astype(BF16), vg.astype(BF16)) for kg, vg in zip(k, v)]

    prev_kv = keys(kvp_ref, cosp_ref[...], sinp_ref[...])
    cur_kv = keys(kvc_ref, cos_c, sin_c)
    band = [(jnp.concatenate([kp, kc], axis=0), jnp.concatenate([vp, vc], axis=0))
            for (kp, vp), (kc, vc) in zip(prev_kv, cur_kv)]

    heads = range(A_HEADS)
    qs = [norm_rope(q_ref[:, p * LANES:(p + 1) * LANES], qn_ref[...], cos_c, sin_c) for p in range(A_PAIRS)]
    qh = [(jnp.where(low_head, qs[h // 2], 0.0) if h % 2 == 0 else jnp.where(low_head, 0.0, qs[h // 2])).astype(BF16)
          for h in heads]
    s = [jnp.where(valid, _dot_nt(qh[h], band[h // A_GROUP][0]), NEG_BIG) for h in heads]
    sink2 = [sink_ref[h] * LOG2E for h in heads]
    m = [jnp.maximum(jnp.max(s[h], axis=-1, keepdims=True), sink2[h]) for h in heads]
    e = [jnp.exp2(s[h] - m[h]) for h in heads]
    denom = [jnp.sum(e[h], axis=-1, keepdims=True) + jnp.exp2(sink2[h] - m[h]) for h in heads]
    outs = [_dot(e[h].astype(BF16), band[h // A_GROUP][1]) / denom[h] for h in heads]
    for p in range(A_PAIRS):
        o = jnp.where(low_head, outs[2 * p], outs[2 * p + 1])
        o_ref[:, p * LANES:(p + 1) * LANES] = (o * _silu(g_ref[:, p * LANES:(p + 1) * LANES])).astype(o_ref.dtype)


def _hgrn_body(q_ref, f_ref, v_ref, g_ref, lbp_ref, gn_ref, cum_ref, o_ref, st_ref, *, layer,
               before_matmuls=None):
    @pl.when(pl.program_id(1) == 0)
    def _():
        st_ref[...] = jnp.zeros_like(st_ref)

    lbp = lbp_ref[...]
    ex = jnp.exp(lbp - jnp.max(lbp, axis=0, keepdims=True))
    sm = ex / jnp.sum(ex, axis=0, keepdims=True)
    lb = jnp.sum(sm[:layer + 1], axis=0, keepdims=True) - sm[0:1]
    lbc = jnp.maximum(lb, LB_FLOOR)
    oml = 1.0 - lb

    s_pos, s_neg = _sigmoid_pair(f_ref[...])
    logf = jnp.log(lbc + oml * s_pos)
    kin = oml * s_neg - (lbc - lb)
    q = _silu(q_ref[...])

    sums = _dot(cum_ref[...], jnp.concatenate(_split_bf16(logf * LOG2E), axis=0))
    blk = q.shape[0]
    seg = lambda n: sums[n * blk:(n + 1) * blk]
    b_incl, b_rest = seg(B_LEVELS), seg(B_LEVELS + 1)
    b_last = b_incl[blk - 1:blk]

    def level_operand(lv):
        size = 1 << lv
        if size % SUBLANES == 0:
            x = jnp.concatenate([(q if n % 2 else kin)[n * size:(n + 1) * size] for n in range(blk // size)], axis=0)
        else:
            odd = ((lax.broadcasted_iota(jnp.int32, (1, SUBLANES, q.shape[1]), 1) >> lv) & 1) == 1
            grouped = (blk // SUBLANES, SUBLANES, q.shape[1])
            x = jnp.where(odd, q.reshape(grouped), kin.reshape(grouped)).reshape(q.shape)
        return (x * jnp.exp2(seg(lv))).astype(BF16)

    xs = [level_operand(lv) for lv in range(B_LEVELS)]
    q_bf, k_bf = q.astype(BF16), kin.astype(BF16)
    q_state = (q * jnp.exp2(b_incl)).astype(BF16)
    k_state = (kin * jnp.exp2(b_rest)).astype(BF16)
    decay = jnp.exp2(b_last)
    if before_matmuls is not None:
        before_matmuls()

    ri = lax.broadcasted_iota(jnp.int32, (blk, blk), 0)
    ci = lax.broadcasted_iota(jnp.int32, (blk, blk), 1)
    at_level = [(((ri >> lv) & 1) == 1) & ((ci >> lv) == (ri >> lv) - 1) for lv in range(B_LEVELS)] + [ri == ci]
    v = v_ref[...]
    heads = range(B_HEADS)
    hs = [slice(h * B_HEAD_DIM, (h + 1) * B_HEAD_DIM) for h in heads]
    att = [jnp.where(at_level[B_LEVELS], _dot_nt(q_bf[:, s], k_bf[:, s]), 0.0) for s in hs]
    for n in range(B_LEVELS):
        att = [jnp.where(at_level[n], _dot_nt(xs[n][:, s], xs[n][:, s]), a) for s, a in zip(hs, att)]
    st = [st_ref[h] for h in heads]
    o = [_dot(att[h].astype(BF16), v[:, hs[h]].astype(BF16)) + _dot_nt(q_state[:, hs[h]], st[h].astype(BF16))
         for h in heads]
    for h in heads:
        st_ref[h] = st[h] * decay[:, hs[h]] + _dot(v[:, hs[h]].T.astype(BF16), k_state[:, hs[h]])
    ms = [jnp.mean(o[h] * o[h], axis=-1, keepdims=True) for h in heads]
    for h in heads:
        y = o[h] * lax.rsqrt(ms[h] + EPS) * gn_ref[...] * _silu(g_ref[:, hs[h]])
        o_ref[:, hs[h]] = y.astype(o_ref.dtype)


def _hgrn_constants():
    blk = B_BLOCK
    i = np.arange(blk)[:, None]
    j = np.arange(blk)[None, :]
    mats = []
    for lv in range(B_LEVELS):
        s = 1 << lv
        same = (i // s) == (j // s)
        odd = (i // s) % 2 == 1
        mats.append(same & np.where(odd, j <= i, j > i))
    mats += [j <= i, j > i]
    cum = np.concatenate(mats, axis=0).astype(np.float32)
    return np.concatenate([cum, cum], axis=1)


N_SWA_IN = 12
N_HGRN_IN = 7


def _even_mixers_body(*refs, layer):
    swa_in, rest = refs[:N_SWA_IN], refs[N_SWA_IN:]
    hgrn_in, (ya_ref, yb_ref, st_ref) = rest[:N_HGRN_IN], rest[N_HGRN_IN:]
    _hgrn_body(*hgrn_in, yb_ref, st_ref, layer=layer, before_matmuls=lambda: _swa_body(*swa_in, ya_ref))


def _even_mixers(qkv, rest_proj, sinks, cos_t, sin_t, qn_l, kn_l, grp, lower_bounds, gn, batch, seq, layer):
    assert A_BLOCK == B_BLOCK
    nb = seq // A_BLOCK
    kv_w = 2 * A_KV_HEADS * A_HEAD_DIM
    kv_col = A_WIDTH // kv_w
    row = lambda b, i: b * nb + i
    prev = lambda i: jnp.maximum(i - 1, 0)
    cum, gn2 = jnp.asarray(_hgrn_constants(), BF16), gn.reshape(1, B_HEAD_DIM)
    wide = lambda c: pl.BlockSpec((A_BLOCK, A_WIDTH), lambda b, i: (row(b, i), c))
    table = lambda at: pl.BlockSpec((A_BLOCK, LANES), lambda b, i: (at(i), 0))
    whole = lambda a: pl.BlockSpec(a.shape, lambda b, i: (0,) * a.ndim)
    swa_specs = [
        pl.BlockSpec(memory_space=pltpu.SMEM), wide(0), wide(0),
        pl.BlockSpec((A_BLOCK, kv_w), lambda b, i: (row(b, prev(i)), kv_col)),
        pl.BlockSpec((A_BLOCK, kv_w), lambda b, i: (row(b, i), kv_col)),
        table(prev), table(prev), table(lambda i: i), table(lambda i: i),
        whole(qn_l), whole(kn_l), whole(grp),
    ]
    assert len(swa_specs) == N_SWA_IN
    hgrn_specs = [wide(1), wide(2), wide(3), wide(4), whole(lower_bounds), whole(gn2), whole(cum)]
    assert len(hgrn_specs) == N_HGRN_IN
    out = jax.ShapeDtypeStruct((batch * seq, A_WIDTH), BF16)
    return pl.pallas_call(
        functools.partial(_even_mixers_body, layer=layer),
        grid=(batch, nb),
        in_specs=swa_specs + hgrn_specs,
        out_specs=[wide(0), wide(0)],
        out_shape=[out, out],
        scratch_shapes=[pltpu.VMEM((B_HEADS, B_HEAD_DIM, B_HEAD_DIM), F32)],
        compiler_params=_cparams(("parallel", "arbitrary")),
        name="even_mixers",
    )(sinks, qkv, rest_proj, qkv, qkv, cos_t, sin_t, cos_t, sin_t, qn_l, kn_l, grp,
      rest_proj, rest_proj, rest_proj, rest_proj, lower_bounds, gn2, cum)


def _stick_tiles(qs, ks, vs, masks, tri, carry_of):
    t = qs[0].shape[0]
    keep = lambda x, mask: x if mask is None else jnp.where(mask, x, 0.0)
    z2s = [_dot_nt(q, k_blk) for q, k_blk in zip(qs, ks)]
    softplus = [jnp.maximum(z2, 0.0) + jnp.log(1.0 + jnp.exp2(-jnp.abs(z2))) * LOG2E for z2 in z2s]
    ps = [keep(p, mask) for p, mask in zip(softplus, masks)]
    incl = _dot(jnp.concatenate([p.astype(BF16) for p in ps], axis=0), tri)
    incls = [incl[n * t:(n + 1) * t] for n in range(len(qs))]
    totals = [x[:, 0:1] for x in incls]
    ws = [keep(jnp.exp2(z2 - incls[n] - carry_of(n, totals)), mask).astype(BF16)
          for n, (z2, mask) in enumerate(zip(z2s, masks))]
    return [_dot(w, v_blk) for w, v_blk in zip(ws, vs)], totals


def _stick_body(q_ref, k_ref, v_ref, g_ref, tri_ref, o_ref):
    step_id = pl.program_id(2)
    t = C_TILE
    n_tiles = q_ref.shape[0] // t
    heads = q_ref.shape[1] // C_HEAD_DIM
    hs = [slice(h * C_HEAD_DIM, (h + 1) * C_HEAD_DIM) for h in range(heads)]
    units = [(u, h) for u in range(n_tiles) for h in range(heads)]
    rows = lambda u: slice(u * t, (u + 1) * t)
    first = step_id * n_tiles
    qs = [q_ref[rows(u), hs[h]] for u, h in units]
    tri = tri_ref[...]
    gates = [_silu(g_ref[rows(u), hs[h]]) for u, h in units]
    row = lax.broadcasted_iota(jnp.int32, (t, t), 0)
    col = lax.broadcasted_iota(jnp.int32, (t, t), 1)

    def key_tiles(kbs):
        starts = [pl.multiple_of(jnp.maximum(kb, 0) * t, t) for kb in kbs]
        return ([k_ref[pl.ds(starts[u], t), hs[h]] for u, h in units],
                [v_ref[pl.ds(starts[u], t), hs[h]] for u, h in units])

    def missing(kbs):
        return [jnp.where(kb >= 0, 0.0, C_NO_TILE) for kb in kbs]

    n = len(units)
    diag = [first + u for u in range(n_tiles)]
    prev = [kb - 1 for kb in diag]
    kd, vd = key_tiles(diag)
    kp, vp = key_tiles(prev)
    no_prev = missing(prev)
    outs, totals = _stick_tiles(
        qs + qs, kd + kp, vd + vp, [col < row] * n + [None] * n, tri,
        lambda m, tot: 0.0 if m < n else tot[m - n] + no_prev[units[m - n][0]])
    accs = tuple(outs[m] + outs[n + m] for m in range(n))
    carries = tuple(totals[m] + totals[n + m] for m in range(n))

    def lowest(carries):
        return functools.reduce(jnp.minimum, [jnp.min(c) for c in carries])

    def more(state):
        back, _, _, min_carry = state
        return (back <= diag[-1]) & (min_carry < C_DONE_BITS)

    def walk(state):
        back, accs, carries, _ = state
        far = [kb - back for kb in diag]
        kn, vn = key_tiles(far)
        gone = missing(far)
        outs, totals = _stick_tiles(qs, kn, vn, [None] * n, tri, lambda m, tot: carries[m] + gone[units[m][0]])
        accs = tuple(a + o for a, o in zip(accs, outs))
        carries = tuple(c + tot for c, tot in zip(carries, totals))
        return back + 1, accs, carries, lowest(carries)

    _, accs, _, _ = lax.while_loop(more, walk, (jnp.int32(2), accs, carries, lowest(carries)))
    for m, (u, h) in enumerate(units):
        o_ref[rows(u), hs[h]] = (accs[m] * gates[m]).astype(o_ref.dtype)


def _stick(qkv, gate, batch, seq):
    tq = C_TILE * C_QTILES_PER_STEP
    nq = seq // tq
    hw = C_HEADS_PER_STEP * C_HEAD_DIM
    groups = C_HEADS // C_HEADS_PER_STEP
    tri = jnp.asarray(np.arange(C_TILE)[:, None] >= np.arange(C_TILE)[None, :], BF16)
    row = lambda b, h, i: b * nq + i
    return pl.pallas_call(
        _stick_body,
        grid=(batch, groups, nq),
        in_specs=[
            pl.BlockSpec((tq, hw), lambda b, h, i: (row(b, h, i), h)),
            pl.BlockSpec((seq, hw), lambda b, h, i: (b, groups + h)),
            pl.BlockSpec((seq, hw), lambda b, h, i: (b, 2 * groups + h)),
            pl.BlockSpec((tq, hw), lambda b, h, i: (row(b, h, i), h)),
            pl.BlockSpec((C_TILE, C_TILE), lambda b, h, i: (0, 0)),
        ],
        out_specs=pl.BlockSpec((tq, hw), lambda b, h, i: (row(b, h, i), h)),
        out_shape=jax.ShapeDtypeStruct((batch * seq, C_WIDTH), BF16),
        compiler_params=_cparams(("parallel", "parallel", "arbitrary")),
        name="stick_breaking_mixer",
    )(qkv, qkv, qkv, gate, tri)


def _swa_constants():
    lane = np.arange(LANES)
    grp = (lane[:, None] // A_HEAD_DIM == lane[None, :] // A_HEAD_DIM).astype(np.float32) / A_HEAD_DIM
    return np.concatenate([grp, grp], axis=0)


def _rope_tables(seq):
    half = A_HALF
    inv_freq = 1.0 / (ROPE_THETA ** (jnp.arange(half, dtype=F32) / half))
    ang = jnp.arange(seq, dtype=F32)[:, None] * inv_freq[None, :]
    cos, sin = jnp.cos(ang), jnp.sin(ang)
    return jnp.tile(cos, (1, 4)), jnp.concatenate([-sin, sin, -sin, sin], axis=1)


def _even_layer(x2, h, batch, seq, w_in, qn, kn, sinks, lower_bounds, gn, w_out, layer, tables, next_ln):
    a_cols = A_WIDTH + 2 * A_KV_HEADS * A_HEAD_DIM
    rest = w_in.shape[2] - a_cols
    qkv, rest_proj = _inproj(h, w_in, layer, [(a_cols, F32), (rest, F32)], tm=1024, tn=a_cols)
    qn_l = (jnp.tile(qn, 2) * (A_HEAD_DIM ** -0.5 * LOG2E)).reshape(1, LANES)
    kn_l = jnp.tile(kn, 2).reshape(1, LANES)
    ya, yb = _even_mixers(qkv, rest_proj, sinks, tables[0], tables[1], qn_l, kn_l,
                          jnp.asarray(_swa_constants(), BF16), lower_bounds, gn, batch, seq, layer)
    return _outproj(ya, yb, 0, 0, w_out, layer, x2, next_ln, tm=512)


def _odd_layer(x2, h, batch, seq, w_in, w_out, layer, next_ln):
    qkv, gate = _inproj(h, w_in, layer, [(3 * C_WIDTH, BF16), (C_WIDTH, F32)],
                        tm=1024, tn=C_WIDTH // 2, scaled_cols=C_WIDTH, scale=C_HEAD_DIM ** -0.5 * LOG2E)
    y = _stick(qkv, gate, batch, seq)
    return _outproj(y, y, 0, 1, w_out, layer, x2, next_ln, tm=512)


def kernel(x, ln_even, w_in_even, q_norm_a, k_norm_a, sinks_a, lower_bounds, g_norm_b, w_out_even,
           ln_odd, w_in_odd, w_out_odd):
    batch, seq, d = x.shape
    depth = ln_even.shape[0] + ln_odd.shape[0]
    x2 = x.reshape(batch * seq, d)
    tables = _rope_tables(seq)
    w_in_even, w_out_even = w_in_even.astype(BF16), w_out_even.astype(BF16)
    w_in_odd, w_out_odd = w_in_odd.astype(BF16), w_out_odd.astype(BF16)
    pre_norm = lambda layer: (ln_even if layer % 2 == 0 else ln_odd)[layer // 2]
    h = _norm(x2, pre_norm(0), tm=512)
    for layer in range(depth):
        e = layer // 2
        next_ln = pre_norm(layer + 1) if layer + 1 < depth else None
        if layer % 2 == 0:
            x2, h = _even_layer(x2, h, batch, seq, w_in_even, q_norm_a[e], k_norm_a[e], sinks_a[e],
                                lower_bounds, g_norm_b[e], w_out_even, e, tables, next_ln)
        else:
            x2, h = _odd_layer(x2, h, batch, seq, w_in_odd, w_out_odd, e, next_ln)
    return x2.reshape(batch, seq, d)
```
